```python
import functools
import jax, jax.numpy as jnp
from jax import lax
import numpy as np

D_MODEL = 2048
BATCH = 4
SEQ = 8192
DEPTH = 2
DEC_BATCH = 16
DEC_SEQ = 32
PAST_LEN = 2048

CHUNK = 64
N_A_LAYERS = DEPTH // 2
N_B_LAYERS = DEPTH - N_A_LAYERS
MIX_WIDTH = D_MODEL
N_MEM = 256
MEM_HEADS = 4
MEM_WIDTH = MIX_WIDTH // 4
MEM_HEAD_DIM = MEM_WIDTH // MEM_HEADS
GMLP_CHUNK = 128
GMLP_GROUPS = 4
GMLP_WIDTH = MIX_WIDTH - MEM_WIDTH
GMLP_GROUP_DIM = GMLP_WIDTH // GMLP_GROUPS
FOX_HEAD_DIM = 128
FOX_HEADS = (MIX_WIDTH - MEM_WIDTH) // FOX_HEAD_DIM
FOX_WIDTH = FOX_HEADS * FOX_HEAD_DIM
FOX_BLOCK = 128
PEER_HEADS = 8
PEER_NKEYS = 128
PEER_EXPERTS = PEER_NKEYS * PEER_NKEYS
PEER_DQ = 256
PEER_TOPK = 16
PEER_BLOCK = 128
EPS = 1e-6
NEG_INF = -1e30

kernel_name = "yoco_gmlp_fox_peer_stream_step"


def rms_norm(x, g):
    xf = x.astype(jnp.float32)
    y = xf * lax.rsqrt(jnp.mean(xf * xf, axis=-1, keepdims=True) + EPS)
    return (y * g.astype(jnp.float32)).astype(x.dtype)


def memory_kv(mem, g, w):
    B, M, _ = mem.shape
    z = rms_norm(mem, g) @ w
    k = z[..., :MEM_WIDTH].reshape(B, M, MEM_HEADS, MEM_HEAD_DIM)
    v = z[..., MEM_WIDTH:].reshape(B, M, MEM_HEADS, MEM_HEAD_DIM)
    return k, v


def mem_attention(q, mk, mv):
    B, T = q.shape[:2]
    s = jnp.einsum('bthd,bmhd->bhtm', q, mk).astype(jnp.float32) * (MEM_HEAD_DIM ** -0.5)
    p = jax.nn.softmax(s, axis=-1).astype(mv.dtype)
    return jnp.einsum('bhtm,bmhd->bthd', p, mv).reshape(B, T, MEM_WIDTH)


def gmlp_mix(u, v, w_s, b_s):
    B, T, _ = u.shape
    L = min(T, GMLP_CHUNK)
    n = T // L
    mask = jnp.tril(jnp.ones((L, L), dtype=bool))
    w = jnp.where(mask, w_s[:, :L, :L], 0.0).astype(v.dtype)
    vg = v.reshape(B, n, L, GMLP_GROUPS, GMLP_GROUP_DIM)
    mixed = jnp.einsum('gts,bnsgc->bntgc', w, vg) + b_s[:, :L].T[None, None, :, :, None].astype(v.dtype)
    return u * mixed.reshape(B, T, GMLP_WIDTH)


def peer_block(hb, w_pq, sub_keys, peer_u, peer_v):
    T = hb.shape[0]
    q = (hb @ w_pq).reshape(T, PEER_HEADS, 2, PEER_DQ // 2)
    s = jnp.einsum('thpc,hpkc->thpk', q, sub_keys).astype(jnp.float32)
    sv, si = lax.top_k(s, PEER_TOPK)
    cand = sv[:, :, 0, :, None] + sv[:, :, 1, None, :]
    cidx = si[:, :, 0, :, None] * PEER_NKEYS + si[:, :, 1, None, :]
    cand = cand.reshape(T, PEER_HEADS, PEER_TOPK * PEER_TOPK)
    cidx = cidx.reshape(T, PEER_HEADS, PEER_TOPK * PEER_TOPK)
    best, pos = lax.top_k(cand, PEER_TOPK)
    eidx = jnp.take_along_axis(cidx, pos, axis=-1)
    gate = jax.nn.softmax(best, axis=-1).astype(hb.dtype)
    u = peer_u[eidx]
    act = jax.nn.gelu(jnp.einsum('thkd,td->thk', u, hb))
    vv = peer_v[eidx]
    return jnp.einsum('thk,thkd->td', gate * act, vv)


def peer_ffn(x, g, w_pq, sub_keys, peer_u, peer_v):
    B, T, D = x.shape
    n = B * T
    pad = (-n) % PEER_BLOCK
    h = rms_norm(x, g).reshape(n, D)
    hb = jnp.pad(h, ((0, pad), (0, 0))).reshape(-1, PEER_BLOCK, D)
    out = lax.map(lambda blk: peer_block(blk, w_pq, sub_keys, peer_u, peer_v), hb)
    return out.reshape(-1, D)[:n].reshape(B, T, D)


def shared_kv(x, g, w_kv, b_f):
    B, T, _ = x.shape
    z = rms_norm(x, g) @ w_kv
    k = z[..., :FOX_WIDTH].reshape(B, T, FOX_HEADS, FOX_HEAD_DIM)
    v = z[..., FOX_WIDTH:2 * FOX_WIDTH].reshape(B, T, FOX_HEADS, FOX_HEAD_DIM)
    logf = jax.nn.log_sigmoid(z[..., 2 * FOX_WIDTH:].astype(jnp.float32) + b_f.astype(jnp.float32))
    return k, v, logf.astype(x.dtype)


def fox_block(q, cq, q_pos, k, v, ck, k_pos):
    s = jnp.einsum('bqhd,bkhd->bhqk', q, k).astype(jnp.float32) * (FOX_HEAD_DIM ** -0.5)
    s = s + (jnp.swapaxes(cq, 1, 2)[:, :, :, None] - jnp.swapaxes(ck, 1, 2)[:, :, None, :])
    mask = k_pos[None, :] <= q_pos[:, None]
    s = jnp.where(mask[None, None], s, NEG_INF)
    p = jax.nn.softmax(s, axis=-1).astype(v.dtype)
    return jnp.einsum('bhqk,bkhd->bqhd', p, v)


def fox_prompt_attend(q, k, v, c):
    B, T, H, d = q.shape
    nb = T // FOX_BLOCK
    pos = jnp.arange(T)
    qb = jnp.swapaxes(q.reshape(B, nb, FOX_BLOCK, H, d), 0, 1)
    cb = jnp.swapaxes(c.reshape(B, nb, FOX_BLOCK, H), 0, 1)
    pb = pos.reshape(nb, FOX_BLOCK)
    o = lax.map(lambda a: fox_block(a[0], a[1], a[2], k, v, c, pos), (qb, cb, pb))
    return jnp.swapaxes(o, 0, 1).reshape(B, T, H * d)


def fox_sample_attend(q, k, v, c):
    B, T = q.shape[:2]
    S = k.shape[1]
    past = S - T
    q_pos = past + jnp.arange(T)
    k_pos = jnp.arange(S)
    return fox_block(q, c[:, past:], q_pos, k, v, c, k_pos).reshape(B, T, FOX_WIDTH)


def layer_a(x, mk, mv, g_mix, w_in, w_s, b_s, g_v, w_o, g_ffn, w_pq, sub_keys, peer_u, peer_v):
    B, T, _ = x.shape
    z = rms_norm(x, g_mix) @ w_in
    uv = jax.nn.gelu(z[..., :2 * GMLP_WIDTH])
    u = uv[..., :GMLP_WIDTH]
    v = rms_norm(uv[..., GMLP_WIDTH:], g_v)
    qm = z[..., 2 * GMLP_WIDTH:].reshape(B, T, MEM_HEADS, MEM_HEAD_DIM)
    mixed = jnp.concatenate([gmlp_mix(u, v, w_s, b_s), mem_attention(qm, mk, mv)], axis=-1)
    x = x + mixed @ w_o
    x = x + peer_ffn(x, g_ffn, w_pq, sub_keys, peer_u, peer_v)
    return x, v


def layer_b(x, fox_fn, mk, mv, g_mix, w_in, w_o, g_ffn, w_pq, sub_keys, peer_u, peer_v):
    B, T, _ = x.shape
    z = rms_norm(x, g_mix) @ w_in
    qf = z[..., :FOX_WIDTH].reshape(B, T, FOX_HEADS, FOX_HEAD_DIM)
    qm = z[..., FOX_WIDTH:].reshape(B, T, MEM_HEADS, MEM_HEAD_DIM)
    mixed = jnp.concatenate([fox_fn(qf), mem_attention(qm, mk, mv)], axis=-1)
    x = x + mixed @ w_o
    x = x + peer_ffn(x, g_ffn, w_pq, sub_keys, peer_u, peer_v)
    return x


def setup_inputs(seed: int = 0) -> dict:
    key = jax.random.key(seed)
    keys = list(jax.random.split(key, 32))
    f32 = jnp.float32

    def nrm(i, shape, scale):
        return jax.random.normal(keys[i], shape, f32) * scale

    d = D_MODEL
    inp = {}
    inp['x_prompt'] = nrm(0, (BATCH, SEQ, d), 1.0)
    inp['x_sample'] = nrm(1, (DEC_BATCH, DEC_SEQ, d), 1.0)
    inp['mem_prompt'] = nrm(2, (BATCH, N_MEM, d), 1.0)
    inp['cache_mem_k'] = nrm(3, (DEPTH, DEC_BATCH, N_MEM, MEM_HEADS, MEM_HEAD_DIM), 1.0)
    inp['cache_mem_v'] = nrm(4, (DEPTH, DEC_BATCH, N_MEM, MEM_HEADS, MEM_HEAD_DIM), 1.0)
    inp['cache_fox_k'] = nrm(5, (DEC_BATCH, PAST_LEN, FOX_HEADS, FOX_HEAD_DIM), 1.0)
    inp['cache_fox_v'] = nrm(6, (DEC_BATCH, PAST_LEN, FOX_HEADS, FOX_HEAD_DIM), 1.0)
    inp['cache_fox_logf'] = jax.nn.log_sigmoid(2.5 + nrm(7, (DEC_BATCH, PAST_LEN, FOX_HEADS), 1.0))
    inp['g_mix'] = 1.0 + nrm(8, (DEPTH, d), 0.1)
    inp['g_ffn'] = 1.0 + nrm(9, (DEPTH, d), 0.1)
    inp['w_o'] = nrm(10, (DEPTH, MIX_WIDTH, d), 0.5 * MIX_WIDTH ** -0.5)
    inp['g_mem'] = 1.0 + nrm(11, (DEPTH, d), 0.1)
    inp['w_mem_kv'] = nrm(12, (DEPTH, d, 2 * MEM_WIDTH), d ** -0.5)
    inp['w_pq'] = nrm(13, (DEPTH, d, PEER_HEADS * PEER_DQ), d ** -0.5)
    inp['peer_sub_keys'] = nrm(14, (DEPTH, PEER_HEADS, 2, PEER_NKEYS, PEER_DQ // 2), (PEER_DQ // 2) ** -0.5)
    inp['peer_u'] = nrm(15, (DEPTH, PEER_EXPERTS, d), d ** -0.5)
    inp['peer_v'] = nrm(16, (DEPTH, PEER_EXPERTS, d), PEER_HEADS ** -0.5)
    inp['w_in_a'] = nrm(17, (N_A_LAYERS, d, 2 * GMLP_WIDTH + MEM_WIDTH), d ** -0.5)
    inp['gmlp_ws'] = nrm(18, (N_A_LAYERS, GMLP_GROUPS, GMLP_CHUNK, GMLP_CHUNK), GMLP_CHUNK ** -0.5)
    inp['gmlp_b'] = 1.0 + nrm(19, (N_A_LAYERS, GMLP_GROUPS, GMLP_CHUNK), 0.1)
    inp['gmlp_gv'] = 1.0 + nrm(20, (N_A_LAYERS, GMLP_WIDTH), 0.1)
    inp['w_in_b'] = nrm(21, (N_B_LAYERS, d, FOX_WIDTH + MEM_WIDTH), d ** -0.5)
    inp['g_kv'] = 1.0 + nrm(22, (d,), 0.1)
    inp['w_kv'] = nrm(23, (d, 2 * FOX_WIDTH + FOX_HEADS), d ** -0.5)
    inp['b_f'] = jax.random.uniform(keys[24], (FOX_HEADS,), f32, 1.0, 4.0)
    inp['g_final'] = 1.0 + nrm(25, (d,), 0.1)
    return inp


def reference(x_prompt, x_sample, mem_prompt, cache_mem_k, cache_mem_v, cache_fox_k, cache_fox_v, cache_fox_logf,
              g_mix, g_ffn, w_o, g_mem, w_mem_kv, w_pq, peer_sub_keys, peer_u, peer_v,
              w_in_a, gmlp_ws, gmlp_b, gmlp_gv, w_in_b, g_kv, w_kv, b_f, g_final):
    def ffn(l):
        return (g_ffn[l], w_pq[l], peer_sub_keys[l], peer_u[l], peer_v[l])

    x = x_prompt
    mem_k_list, mem_v_list = [], []
    for l in range(DEPTH):
        mk, mv = memory_kv(mem_prompt, g_mem[l], w_mem_kv[l])
        mem_k_list.append(mk)
        mem_v_list.append(mv)
        if l < N_A_LAYERS:
            x, _ = layer_a(x, mk, mv, g_mix[l], w_in_a[l], gmlp_ws[l], gmlp_b[l], gmlp_gv[l], w_o[l], *ffn(l))
            if l == N_A_LAYERS - 1:
                fox_k_p, fox_v_p, fox_logf_p = shared_kv(x, g_kv, w_kv, b_f)
                fox_c_p = jnp.cumsum(fox_logf_p.astype(jnp.float32), axis=1)
        else:
            fox_fn = functools.partial(fox_prompt_attend, k=fox_k_p, v=fox_v_p, c=fox_c_p)
            x = layer_b(x, fox_fn, mk, mv, g_mix[l], w_in_b[l - N_A_LAYERS], w_o[l], *ffn(l))
    y_prompt = rms_norm(x, g_final)
    new_mem_k_prompt = jnp.stack(mem_k_list)
    new_mem_v_prompt = jnp.stack(mem_v_list)

    x = x_sample
    gmlp_v_list = []
    for l in range(DEPTH):
        mk, mv = cache_mem_k[l], cache_mem_v[l]
        if l < N_A_LAYERS:
            x, v_rows = layer_a(x, mk, mv, g_mix[l], w_in_a[l], gmlp_ws[l], gmlp_b[l], gmlp_gv[l], w_o[l], *ffn(l))
            gmlp_v_list.append(v_rows)
            if l == N_A_LAYERS - 1:
                fox_k_s, fox_v_s, fox_logf_s = shared_kv(x, g_kv, w_kv, b_f)
                k_all = jnp.concatenate([cache_fox_k, fox_k_s], axis=1)
                v_all = jnp.concatenate([cache_fox_v, fox_v_s], axis=1)
                c_all = jnp.cumsum(jnp.concatenate([cache_fox_logf, fox_logf_s], axis=1).astype(jnp.float32), axis=1)
        else:
            fox_fn = functools.partial(fox_sample_attend, k=k_all, v=v_all, c=c_all)
            x = layer_b(x, fox_fn, mk, mv, g_mix[l], w_in_b[l - N_A_LAYERS], w_o[l], *ffn(l))
    y_sample = rms_norm(x, g_final)
    new_gmlp_v_sample = jnp.stack(gmlp_v_list)

    return (y_prompt, y_sample, new_mem_k_prompt, new_mem_v_prompt, fox_k_p, fox_v_p, fox_logf_p,
            fox_k_s, fox_v_s, fox_logf_s, new_gmlp_v_sample)
```

```python
import functools

import jax
import jax.numpy as jnp
from jax import lax
from jax.experimental import pallas as pl
from jax.experimental.pallas import tpu as pltpu

EPS = 1e-6
NEG_INF = -1e30
LANES = 128
HEAD_DIM = 128
MEM_HEADS = 4
MEM_WIDTH = MEM_HEADS * HEAD_DIM
GMLP_GROUPS = 4
GMLP_CHUNK = 128
PEER_HEADS = 8
PEER_TOPK = 16
PEER_NKEYS = 128
PEER_PAIRS = PEER_HEADS * PEER_TOPK
FOX_HEADS_PAD = 16
VMEM_LIMIT = 56 * 1024 * 1024
BF16 = jnp.bfloat16
F32 = jnp.float32
NT = (((1,), (1,)), ((), ()))


def _params(*sem):
    return pltpu.CompilerParams(dimension_semantics=sem, vmem_limit_bytes=VMEM_LIMIT)


def _rms(x, g):
    return x * lax.rsqrt(jnp.mean(x * x, axis=-1, keepdims=True) + EPS) * g


def _norm_matmul_kernel(x_ref, g_ref, w_ref, o_ref, xn_ref):
    @pl.when(pl.program_id(1) == 0)
    def _():
        xn_ref[...] = _rms(x_ref[...], g_ref[...]).astype(BF16)

    o_ref[...] = jnp.dot(xn_ref[...], w_ref[...], preferred_element_type=F32).astype(o_ref.dtype)


def norm_matmul(x, g, w, *, tn=512, out_dtype=F32):
    n, d = x.shape
    dout = w.shape[1]
    tm = min(512, n)
    tn = min(tn, dout)
    assert n % tm == 0 and dout % tn == 0
    return pl.pallas_call(
        _norm_matmul_kernel,
        grid=(n // tm, dout // tn),
        in_specs=[
            pl.BlockSpec((tm, d), lambda i, j: (i, 0)),
            pl.BlockSpec((1, d), lambda i, j: (0, 0)),
            pl.BlockSpec((d, tn), lambda i, j: (0, j)),
        ],
        out_specs=pl.BlockSpec((tm, tn), lambda i, j: (i, j)),
        out_shape=jax.ShapeDtypeStruct((n, dout), out_dtype),
        scratch_shapes=[pltpu.VMEM((tm, d), BF16)],
        compiler_params=_params("parallel", "arbitrary"),
        name="norm_matmul",
    )(x, g.reshape(1, d), w)


def _gmlp_kernel(zu_ref, zv_ref, w_ref, b_ref, gv_ref, mix_ref, v_ref, *, chunk, n_chunks, gdim):
    tril = (lax.broadcasted_iota(jnp.int32, (chunk, chunk), 1)
            <= lax.broadcasted_iota(jnp.int32, (chunk, chunk), 0))
    for c in range(n_chunks):
        rows = slice(c * chunk, (c + 1) * chunk)
        u = jax.nn.gelu(zu_ref[rows, :])
        v = _rms(jax.nn.gelu(zv_ref[rows, :]), gv_ref[...])
        v_ref[rows, :] = v
        v16 = v.astype(BF16)
        for g in range(GMLP_GROUPS):
            cols = slice(g * gdim, (g + 1) * gdim)
            w = jnp.where(tril, w_ref[g], 0.0).astype(BF16)
            mixed = jnp.dot(w, v16[:, cols], preferred_element_type=F32) + b_ref[g]
            mix_ref[rows, cols] = (u[:, cols] * mixed).astype(BF16)


def gmlp_mix(z, w_s, b_s, g_v, *, chunk, width):
    n = z.shape[0]
    tm = min(512, n)
    assert n % tm == 0 and tm % chunk == 0
    gdim = width // GMLP_GROUPS
    w = w_s[:, :chunk, :chunk]
    b = b_s[:, :chunk].reshape(GMLP_GROUPS, chunk, 1)
    kern = functools.partial(_gmlp_kernel, chunk=chunk, n_chunks=tm // chunk, gdim=gdim)
    return pl.pallas_call(
        kern,
        grid=(n // tm,),
        in_specs=[
            pl.BlockSpec((tm, width), lambda i: (i, 0)),
            pl.BlockSpec((tm, width), lambda i: (i, 1)),
            pl.BlockSpec((GMLP_GROUPS, chunk, chunk), lambda i: (0, 0, 0)),
            pl.BlockSpec((GMLP_GROUPS, chunk, 1), lambda i: (0, 0, 0)),
            pl.BlockSpec((1, width), lambda i: (0, 0)),
        ],
        out_specs=[
            pl.BlockSpec((tm, width), lambda i: (i, 0)),
            pl.BlockSpec((tm, width), lambda i: (i, 0)),
        ],
        out_shape=[jax.ShapeDtypeStruct((n, width), BF16), jax.ShapeDtypeStruct((n, width), F32)],
        compiler_params=_params("parallel"),
        name="gmlp_mix",
    )(z, z, w, b, g_v.reshape(1, width))


def _mix_out_kernel(part_ref, qm_ref, mk_ref, mv_ref, wo_ref, x_ref, o_ref, *, n_sub, t_sub, width):
    scale = HEAD_DIM ** -0.5
    subs = []
    for bi in range(n_sub):
        rows = slice(bi * t_sub, (bi + 1) * t_sub)
        heads = []
        for h in range(MEM_HEADS):
            cols = slice(h * HEAD_DIM, (h + 1) * HEAD_DIM)
            q = qm_ref[rows, cols].astype(BF16)
            s = lax.dot_general(q, mk_ref[bi, :, cols], NT, preferred_element_type=F32) * scale
            s = s - jnp.max(s, axis=-1, keepdims=True)
            p = jnp.exp(s)
            p = (p / jnp.sum(p, axis=-1, keepdims=True)).astype(BF16)
            heads.append(jnp.dot(p, mv_ref[bi, :, cols], preferred_element_type=F32))
        subs.append(jnp.concatenate(heads, axis=1))
    mem = (subs[0] if n_sub == 1 else jnp.concatenate(subs, axis=0)).astype(BF16)
    y = jnp.dot(part_ref[...], wo_ref[:width, :], preferred_element_type=F32)
    y = y + jnp.dot(mem, wo_ref[width:, :], preferred_element_type=F32)
    o_ref[...] = x_ref[...] + y


def mix_out(part, z, q_block, mk, mv, w_o, x, *, seq):
    n, d = x.shape
    width = part.shape[1]
    n_mem = mk.shape[1]
    tm = min(256, n)
    assert n % tm == 0
    if seq >= tm:
        assert seq % tm == 0
        n_sub, t_sub = 1, tm
        kv_map = lambda i: ((i * tm) // seq, 0, 0)
    else:
        assert tm % seq == 0
        n_sub, t_sub = tm // seq, seq
        kv_map = lambda i: (i, 0, 0)
    kern = functools.partial(_mix_out_kernel, n_sub=n_sub, t_sub=t_sub, width=width)
    return pl.pallas_call(
        kern,
        grid=(n // tm,),
        in_specs=[
            pl.BlockSpec((tm, width), lambda i: (i, 0)),
            pl.BlockSpec((tm, MEM_WIDTH), lambda i: (i, q_block)),
            pl.BlockSpec((n_sub, n_mem, MEM_WIDTH), kv_map),
            pl.BlockSpec((n_sub, n_mem, MEM_WIDTH), kv_map),
            pl.BlockSpec((width + MEM_WIDTH, d), lambda i: (0, 0)),
            pl.BlockSpec((tm, d), lambda i: (i, 0)),
        ],
        out_specs=pl.BlockSpec((tm, d), lambda i: (i, 0)),
        out_shape=jax.ShapeDtypeStruct((n, d), F32),
        compiler_params=_params("parallel"),
        name="mix_out",
    )(part, z, mk, mv, w_o, x)


def _log_sigmoid(x):
    return jnp.minimum(x, 0.0) - jnp.log1p(jnp.exp(-jnp.abs(x)))


def _cumsum_kernel(*refs, has_prev, t_new):
    if has_prev:
        prev_ref, raw_ref, b_ref, c_ref, lf_ref = refs
    else:
        raw_ref, b_ref, c_ref, lf_ref = refs
    raw = raw_ref[0]
    lf = _log_sigmoid(raw + b_ref[...])
    lf = jnp.where(lax.broadcasted_iota(jnp.int32, lf.shape, 1) < t_new, lf, 0.0)
    lf_ref[0] = lf
    full = jnp.concatenate([prev_ref[0], lf], axis=1) if has_prev else lf
    total = full.shape[1]
    lane = lax.broadcasted_iota(jnp.int32, full.shape, 1)
    shift = 1
    while shift < total:
        full = full + jnp.where(lane >= shift, pltpu.roll(full, shift, 1), 0.0)
        shift *= 2
    c_ref[0] = full


def logf_cumsum(raw_t, b_col, t_new, prev_t=None):
    nb, hp, t_pad = raw_t.shape
    past = 0 if prev_t is None else prev_t.shape[2]
    kern = functools.partial(_cumsum_kernel, has_prev=prev_t is not None, t_new=t_new)
    in_specs = [pl.BlockSpec((1, hp, t_pad), lambda i: (i, 0, 0)), pl.BlockSpec((hp, 1), lambda i: (0, 0))]
    args = [raw_t, b_col]
    if prev_t is not None:
        in_specs = [pl.BlockSpec((1, hp, past), lambda i: (i, 0, 0))] + in_specs
        args = [prev_t] + args
    return pl.pallas_call(
        kern,
        grid=(nb,),
        in_specs=in_specs,
        out_specs=[
            pl.BlockSpec((1, hp, past + t_pad), lambda i: (i, 0, 0)),
            pl.BlockSpec((1, hp, t_pad), lambda i: (i, 0, 0)),
        ],
        out_shape=[jax.ShapeDtypeStruct((nb, hp, past + t_pad), F32), jax.ShapeDtypeStruct((nb, hp, t_pad), F32)],
        compiler_params=_params("parallel"),
        name="logf_cumsum",
    )(*args)


def _fox_kernel(q_ref, k_ref, v_ref, ck_ref, cq_ref, o_ref, m_ref, l_ref, acc_ref, *, tq, tk):
    qi = pl.program_id(2)
    ki = pl.program_id(3)

    @pl.when(ki == 0)
    def _():
        m_ref[...] = jnp.full(m_ref.shape, NEG_INF, F32)
        l_ref[...] = jnp.zeros(l_ref.shape, F32)
        acc_ref[...] = jnp.zeros(acc_ref.shape, F32)

    @pl.when(ki * tk <= qi * tq + tq - 1)
    def _():
        s = lax.dot_general(q_ref[...], k_ref[...], NT, preferred_element_type=F32) * (HEAD_DIM ** -0.5)
        s = s - (ck_ref[0, 0] - cq_ref[0, 0, :, 0:1])
        qpos = qi * tq + lax.broadcasted_iota(jnp.int32, (tq, tk), 0)
        kpos = ki * tk + lax.broadcasted_iota(jnp.int32, (tq, tk), 1)
        s = jnp.where(kpos <= qpos, s, NEG_INF)
        m_prev = m_ref[...]
        m_new = jnp.maximum(m_prev, jnp.max(s, axis=-1, keepdims=True))
        alpha = jnp.exp(m_prev - m_new)
        p = jnp.exp(s - m_new)
        l_ref[...] = alpha * l_ref[...] + jnp.sum(p, axis=-1, keepdims=True)
        acc_ref[...] = alpha * acc_ref[...] + jnp.dot(p.astype(BF16), v_ref[...], preferred_element_type=F32)
        m_ref[...] = m_new

    @pl.when(ki == pl.num_programs(3) - 1)
    def _():
        o_ref[...] = (acc_ref[...] / l_ref[...]).astype(o_ref.dtype)


def fox_prompt(zq, k, v, c, *, batch, seq, heads):
    n = zq.shape[0]
    tq = tk = min(512, seq)
    nq, nk = seq // tq, seq // tk
    last_k = lambda qi: (qi * tq + tq - 1) // tk
    kern = functools.partial(_fox_kernel, tq=tq, tk=tk)
    kv_map = lambda b, h, qi, ki: (b * nk + jnp.minimum(ki, last_k(qi)), h)
    return pl.pallas_call(
        kern,
        grid=(batch, heads, nq, nk),
        in_specs=[
            pl.BlockSpec((tq, HEAD_DIM), lambda b, h, qi, ki: (b * nq + qi, h)),
            pl.BlockSpec((tk, HEAD_DIM), kv_map),
            pl.BlockSpec((tk, HEAD_DIM), kv_map),
            pl.BlockSpec((1, 1, 1, tk), lambda b, h, qi, ki: (b, h, 0, jnp.minimum(ki, last_k(qi)))),
            pl.BlockSpec((1, 1, 1, tq), lambda b, h, qi, ki: (b, h, 0, qi)),
        ],
        out_specs=pl.BlockSpec((tq, HEAD_DIM), lambda b, h, qi, ki: (b * nq + qi, h)),
        out_shape=jax.ShapeDtypeStruct((n, heads * HEAD_DIM), BF16),
        scratch_shapes=[pltpu.VMEM((tq, 1), F32), pltpu.VMEM((tq, 1), F32), pltpu.VMEM((tq, HEAD_DIM), F32)],
        compiler_params=_params("parallel", "parallel", "parallel", "arbitrary"),
        name="fox_prompt",
    )(zq, k, v, c, c)


def _fox_sample_kernel(q_ref, kc_ref, vc_ref, kn_ref, vn_ref, c_ref, o_ref, *, past, t_new):
    scale = HEAD_DIM ** -0.5
    q = q_ref[...].astype(BF16)
    c = c_ref[0, 0]
    c_ref0 = c[:, past:past + 1]
    s_old = lax.dot_general(q, kc_ref[0].astype(BF16), NT, preferred_element_type=F32) * scale
    s_old = s_old - (c[:, :past] - c_ref0)
    s_new = lax.dot_general(q, kn_ref[...].astype(BF16), NT, preferred_element_type=F32) * scale
    s_new = s_new - (c[:, past:past + t_new] - c_ref0)
    causal = (lax.broadcasted_iota(jnp.int32, (t_new, t_new), 1)
              <= lax.broadcasted_iota(jnp.int32, (t_new, t_new), 0))
    s_new = jnp.where(causal, s_new, NEG_INF)
    m = jnp.maximum(jnp.max(s_old, axis=-1, keepdims=True), jnp.max(s_new, axis=-1, keepdims=True))
    p_old = jnp.exp(s_old - m)
    p_new = jnp.exp(s_new - m)
    denom = jnp.sum(p_old, axis=-1, keepdims=True) + jnp.sum(p_new, axis=-1, keepdims=True)
    acc = jnp.dot(p_old.astype(BF16), vc_ref[0].astype(BF16), preferred_element_type=F32)
    acc = acc + jnp.dot(p_new.astype(BF16), vn_ref[...].astype(BF16), preferred_element_type=F32)
    o_ref[...] = (acc / denom).astype(o_ref.dtype)


def fox_sample(zq, k_cache, v_cache, k_new, v_new, c, *, batch, t_new, heads):
    past = k_cache.shape[1]
    c_len = c.shape[3]
    kern = functools.partial(_fox_sample_kernel, past=past, t_new=t_new)
    return pl.pallas_call(
        kern,
        grid=(batch, heads),
        in_specs=[
            pl.BlockSpec((t_new, HEAD_DIM), lambda b, h: (b, h)),
            pl.BlockSpec((1, past, HEAD_DIM), lambda b, h: (b, 0, h)),
            pl.BlockSpec((1, past, HEAD_DIM), lambda b, h: (b, 0, h)),
            pl.BlockSpec((t_new, HEAD_DIM), lambda b, h: (b, h)),
            pl.BlockSpec((t_new, HEAD_DIM), lambda b, h: (b, h)),
            pl.BlockSpec((1, 1, 1, c_len), lambda b, h: (b, h, 0, 0)),
        ],
        out_specs=pl.BlockSpec((t_new, HEAD_DIM), lambda b, h: (b, h)),
        out_shape=jax.ShapeDtypeStruct((batch * t_new, heads * HEAD_DIM), BF16),
        compiler_params=_params("parallel", "parallel"),
        name="fox_sample",
    )(zq, k_cache, v_cache, k_new, v_new, c)


def _top_rows(vals, count, payload=None):
    n_rows = vals.shape[0]
    rows = lax.broadcasted_iota(jnp.int32, vals.shape, 0)
    out_v, out_i = [], []
    for _ in range(count):
        m = jnp.max(vals, axis=0, keepdims=True)
        idx = jnp.min(jnp.where(vals == m, rows, n_rows), axis=0, keepdims=True)
        hit = rows == idx
        out_v.append(m)
        out_i.append(idx if payload is None else jnp.max(jnp.where(hit, payload, -1), axis=0, keepdims=True))
        vals = jnp.where(hit, -jnp.inf, vals)
    return out_v, out_i


def _peer_route_kernel(q_ref, keys_ref, eidx_ref, gate_ref, e_scr, g_scr):
    h = pl.program_id(1)
    q = q_ref[...].astype(BF16)
    half_v, half_i = [], []
    for p in range(2):
        keys = keys_ref[0, p].astype(BF16)
        s = lax.dot_general(keys, q[:, p * LANES:(p + 1) * LANES], NT, preferred_element_type=F32)
        sv, si = _top_rows(s, PEER_TOPK)
        half_v.append(sv)
        half_i.append(si)
    v1 = jnp.concatenate(half_v[1], axis=0)
    i1 = jnp.concatenate(half_i[1], axis=0)
    cand = jnp.concatenate([half_v[0][a] + v1 for a in range(PEER_TOPK)], axis=0)
    cidx = jnp.concatenate([half_i[0][a] * PEER_NKEYS + i1 for a in range(PEER_TOPK)], axis=0)
    best, eidx = _top_rows(cand, PEER_TOPK, payload=cidx)
    best = jnp.concatenate(best, axis=0)
    e = jnp.exp(best - jnp.max(best, axis=0, keepdims=True))
    gate = e / jnp.sum(e, axis=0, keepdims=True)
    row0 = pl.multiple_of(h * PEER_TOPK, PEER_TOPK)
    e_scr[pl.ds(row0, PEER_TOPK), :] = jnp.concatenate(eidx, axis=0)
    g_scr[pl.ds(row0, PEER_TOPK), :] = gate

    @pl.when(h == pl.num_programs(1) - 1)
    def _():
        eidx_ref[...] = e_scr[...].T
        gate_ref[...] = g_scr[...].T


def peer_route(q, sub_keys):
    n = q.shape[0]
    tt = min(256, n)
    assert n % tt == 0
    return pl.pallas_call(
        _peer_route_kernel,
        grid=(n // tt, PEER_HEADS),
        in_specs=[
            pl.BlockSpec((tt, 2 * LANES), lambda i, h: (i, h)),
            pl.BlockSpec((1, 2, PEER_NKEYS, LANES), lambda i, h: (h, 0, 0, 0)),
        ],
        out_specs=[
            pl.BlockSpec((tt, PEER_PAIRS), lambda i, h: (i, 0)),
            pl.BlockSpec((tt, PEER_PAIRS), lambda i, h: (i, 0)),
        ],
        out_shape=[jax.ShapeDtypeStruct((n, PEER_PAIRS), jnp.int32), jax.ShapeDtypeStruct((n, PEER_PAIRS), F32)],
        scratch_shapes=[pltpu.VMEM((PEER_PAIRS, tt), jnp.int32), pltpu.VMEM((PEER_PAIRS, tt), F32)],
        compiler_params=_params("parallel", "arbitrary"),
        name="peer_route",
    )(q, sub_keys)


def _peer_apply_kernel(idx_ref, x_ref, gate_ref, g_ref, gf_ref, uv_hbm, o_ref, buf, sem, *, group, d, final_norm):
    def issue(t, slot):
        for k in range(PEER_PAIRS):
            pltpu.make_async_copy(uv_hbm.at[pl.ds(idx_ref[t, k], 1)], buf.at[slot, pl.ds(k, 1)], sem.at[slot]).start()

    def wait(slot):
        pltpu.make_async_copy(uv_hbm.at[pl.ds(0, PEER_PAIRS)], buf.at[slot], sem.at[slot]).wait()

    issue(0, 0)
    x = x_ref[...]
    h16 = _rms(x, g_ref[...]).astype(BF16)
    gate = gate_ref[...]
    rows = lax.broadcasted_iota(jnp.int32, (group, PEER_PAIRS), 0)
    o_ref[...] = x

    def body(t, carry):
        slot = t % 2
        wait(slot)

        @pl.when(t + 1 < group)
        def _():
            issue(t + 1, 1 - slot)

        u16 = buf[slot, :, :d].astype(BF16)
        act = lax.dot_general(h16, u16, NT, preferred_element_type=F32)
        coef = jnp.where(rows == t, gate * jax.nn.gelu(act), 0.0).astype(BF16)
        v16 = buf[slot, :, d:].astype(BF16)
        o_ref[...] += jnp.dot(coef, v16, preferred_element_type=F32)
        return carry

    lax.fori_loop(0, group, body, 0)
    if final_norm:
        o_ref[...] = _rms(o_ref[...], gf_ref[...])


def peer_apply(x, g, eidx, gate, uv, g_final=None, *, group=8):
    n, d = x.shape
    assert n % group == 0
    final_norm = g_final is not None
    gf = (g_final if final_norm else g).reshape(1, d)
    kern = functools.partial(_peer_apply_kernel, group=group, d=d, final_norm=final_norm)
    return pl.pallas_call(
        kern,
        grid=(n // group,),
        in_specs=[
            pl.BlockSpec((group, PEER_PAIRS), lambda i: (i, 0), memory_space=pltpu.SMEM),
            pl.BlockSpec((group, d), lambda i: (i, 0)),
            pl.BlockSpec((group, PEER_PAIRS), lambda i: (i, 0)),
            pl.BlockSpec((1, d), lambda i: (0, 0)),
            pl.BlockSpec((1, d), lambda i: (0, 0)),
            pl.BlockSpec(memory_space=pl.ANY),
        ],
        out_specs=pl.BlockSpec((group, d), lambda i: (i, 0)),
        out_shape=jax.ShapeDtypeStruct((n, d), F32),
        scratch_shapes=[pltpu.VMEM((2, PEER_PAIRS, 2 * d), F32), pltpu.SemaphoreType.DMA((2,))],
        compiler_params=_params("arbitrary"),
        name="peer_apply",
    )(eidx, x, gate, g.reshape(1, d), gf, uv)


def peer_ffn(x, g, w_pq16, sub_keys, uv, g_final=None):
    q = norm_matmul(x, g, w_pq16)
    eidx, gate = peer_route(q, sub_keys)
    return peer_apply(x, g, eidx, gate, uv, g_final)


def _pad_rows(a, rows):
    return jnp.pad(a, ((0, rows - a.shape[0]),) + ((0, 0),) * (a.ndim - 1))


def _forget_inputs(zf, batch, t_new, t_pad):
    raw = zf[:, :FOX_HEADS_PAD].reshape(batch, t_new, FOX_HEADS_PAD)
    raw = jnp.swapaxes(raw, 1, 2)
    return jnp.pad(raw, ((0, 0), (0, 0), (0, t_pad - t_new)))


def kernel(x_prompt, x_sample, mem_prompt, cache_mem_k, cache_mem_v, cache_fox_k, cache_fox_v, cache_fox_logf, g_mix, g_ffn, w_o, g_mem, w_mem_kv, w_pq, peer_sub_keys, peer_u, peer_v, w_in_a, gmlp_ws, gmlp_b, gmlp_gv, w_in_b, g_kv, w_kv, b_f, g_final):
    batch, seq, d = x_prompt.shape
    dec_batch, dec_seq, _ = x_sample.shape
    depth = g_mix.shape[0]
    n_a = w_in_a.shape[0]
    n_mem = mem_prompt.shape[1]
    past = cache_fox_k.shape[1]
    heads = cache_fox_k.shape[2]
    fox_width = heads * HEAD_DIM
    gmlp_width = gmlp_gv.shape[1]
    assert gmlp_width == fox_width and d == fox_width + MEM_WIDTH

    w_o16 = w_o.astype(BF16)
    w_pq16 = w_pq.astype(BF16)
    w_in_a16 = w_in_a.astype(BF16)
    w_in_b16 = w_in_b.astype(BF16)
    w_mem16 = w_mem_kv.astype(BF16)
    w_k16 = w_kv[:, :fox_width].astype(BF16)
    w_v16 = w_kv[:, fox_width:2 * fox_width].astype(BF16)
    w_f16 = jnp.pad(w_kv[:, 2 * fox_width:], ((0, 0), (0, LANES - heads))).astype(BF16)
    b_col = jnp.pad(b_f, (0, FOX_HEADS_PAD - heads)).reshape(FOX_HEADS_PAD, 1)
    uv = [jnp.concatenate([peer_u[l], peer_v[l]], axis=1) for l in range(depth)]

    def ffn(x, l, final=False):
        return peer_ffn(x, g_ffn[l], w_pq16[l], peer_sub_keys[l], uv[l], g_final if final else None)

    def shared_kv(x):
        k = norm_matmul(x, g_kv, w_k16)
        v = norm_matmul(x, g_kv, w_v16)
        zf = norm_matmul(x, g_kv, w_f16)
        return k, v, zf

    def run_group(x3, mem_k, mem_v, chunk, fox_fn):
        b, t, _ = x3.shape
        x = x3.reshape(b * t, d)
        gmlp_v = []
        extras = None
        for l in range(depth):
            if l < n_a:
                z = norm_matmul(x, g_mix[l], w_in_a16[l])
                part, v_rows = gmlp_mix(z, gmlp_ws[l], gmlp_b[l], gmlp_gv[l], chunk=chunk, width=gmlp_width)
                gmlp_v.append(v_rows.reshape(b, t, gmlp_width))
                x = mix_out(part, z, 2 * gmlp_width // MEM_WIDTH, mem_k[l], mem_v[l], w_o16[l], x, seq=t)
                x = ffn(x, l)
                if l == n_a - 1:
                    kvf = shared_kv(x)
            else:
                z = norm_matmul(x, g_mix[l], w_in_b16[l - n_a])
                part, extras = fox_fn(z, kvf, extras)
                x = mix_out(part, z, fox_width // MEM_WIDTH, mem_k[l], mem_v[l], w_o16[l], x, seq=t)
                x = ffn(x, l, final=(l == depth - 1))
        return x.reshape(b, t, d), gmlp_v, extras

    mem2d = mem_prompt.reshape(batch * n_mem, d)
    mem_k_p, mem_v_p = [], []
    for l in range(depth):
        zkv = norm_matmul(mem2d, g_mem[l], w_mem16[l])
        mem_k_p.append(zkv[:, :MEM_WIDTH].reshape(batch, n_mem, MEM_WIDTH))
        mem_v_p.append(zkv[:, MEM_WIDTH:].reshape(batch, n_mem, MEM_WIDTH))

    def fox_p(z, kvf, extras):
        if extras is None:
            k, v, zf = kvf
            c, lf = logf_cumsum(_forget_inputs(zf, batch, seq, seq), b_col, seq)
            extras = (k, v, lf, c[:, :heads].reshape(batch, heads, 1, seq), k.astype(BF16), v.astype(BF16))
        k, v, lf, c, k16, v16 = extras
        return fox_prompt(z.astype(BF16), k16, v16, c, batch=batch, seq=seq, heads=heads), extras

    y_p, _, ex_p = run_group(x_prompt, [a.astype(BF16) for a in mem_k_p], [a.astype(BF16) for a in mem_v_p],
                             min(seq, GMLP_CHUNK), fox_p)
    fox_k_p = ex_p[0].reshape(batch, seq, heads, HEAD_DIM)
    fox_v_p = ex_p[1].reshape(batch, seq, heads, HEAD_DIM)
    fox_logf_p = jnp.swapaxes(ex_p[2][:, :heads, :seq], 1, 2)
    new_mem_k = jnp.stack(mem_k_p).reshape(depth, batch, n_mem, MEM_HEADS, HEAD_DIM)
    new_mem_v = jnp.stack(mem_v_p).reshape(depth, batch, n_mem, MEM_HEADS, HEAD_DIM)

    t_pad = -(-dec_seq // LANES) * LANES
    prev_t = jnp.pad(jnp.swapaxes(cache_fox_logf, 1, 2), ((0, 0), (0, FOX_HEADS_PAD - heads), (0, 0)))
    k_cache = cache_fox_k.reshape(dec_batch, past, fox_width)
    v_cache = cache_fox_v.reshape(dec_batch, past, fox_width)

    def fox_s(z, kvf, extras):
        if extras is None:
            k, v, zf = kvf
            c, lf = logf_cumsum(_forget_inputs(zf, dec_batch, dec_seq, t_pad), b_col, dec_seq, prev_t)
            extras = (k, v, lf, c[:, :heads].reshape(dec_batch, heads, 1, past + t_pad))
        k, v, lf, c = extras
        return fox_sample(z, k_cache, v_cache, k, v, c, batch=dec_batch, t_new=dec_seq, heads=heads), extras

    mem_k_s = [cache_mem_k[l].reshape(dec_batch, n_mem, MEM_WIDTH).astype(BF16) for l in range(depth)]
    mem_v_s = [cache_mem_v[l].reshape(dec_batch, n_mem, MEM_WIDTH).astype(BF16) for l in range(depth)]
    y_s, gmlp_v_s, ex_s = run_group(x_sample, mem_k_s, mem_v_s, min(dec_seq, GMLP_CHUNK), fox_s)
    fox_k_s = ex_s[0].reshape(dec_batch, dec_seq, heads, HEAD_DIM)
    fox_v_s = ex_s[1].reshape(dec_batch, dec_seq, heads, HEAD_DIM)
    fox_logf_s = jnp.swapaxes(ex_s[2][:, :heads, :dec_seq], 1, 2)

    return (y_p, y_s, new_mem_k, new_mem_v, fox_k_p, fox_v_p, fox_logf_p,
            fox_k_s, fox_v_s, fox_logf_s, jnp.stack(gmlp_v_s))
```

```python
import functools

import jax
import jax.numpy as jnp
from jax import lax
from jax.experimental import pallas as pl
from jax.experimental.pallas import tpu as pltpu

EPS = 1e-6
NEG_INF = -1e30
LANES = 128
HEAD_DIM = 128
MEM_HEADS = 4
MEM_WIDTH = MEM_HEADS * HEAD_DIM
GMLP_GROUPS = 4
GMLP_CHUNK = 128
PEER_HEADS = 8
PEER_TOPK = 16
PEER_NKEYS = 128
PEER_PAIRS = PEER_HEADS * PEER_TOPK
PEER_UNROLL = 16
FOX_HEADS_PAD = 16
VMEM_LIMIT = 56 * 1024 * 1024
BF16 = jnp.bfloat16
F32 = jnp.float32
NT = (((1,), (1,)), ((), ()))


def _params(*sem):
    return pltpu.CompilerParams(dimension_semantics=sem, vmem_limit_bytes=VMEM_LIMIT)


def _rms(x, g):
    return x * lax.rsqrt(jnp.mean(x * x, axis=-1, keepdims=True) + EPS) * g


def _norm_matmul_kernel(x_ref, g_ref, w_ref, o_ref, *rest):
    xn_ref = rest[-1]

    @pl.when(pl.program_id(1) == 0)
    def _():
        h = _rms(x_ref[...], g_ref[...])
        xn_ref[...] = h.astype(BF16)
        if len(rest) == 2:
            rest[0][...] = h

    o_ref[...] = jnp.dot(xn_ref[...], w_ref[...], preferred_element_type=F32).astype(o_ref.dtype)


def norm_matmul(x, g, w, *, tn=512, out_dtype=F32, emit_norm=False):
    n, d = x.shape
    dout = w.shape[1]
    tm = min(512, n)
    tn = min(tn, dout)
    assert n % tm == 0 and dout % tn == 0
    out_specs = [pl.BlockSpec((tm, tn), lambda i, j: (i, j))]
    out_shape = [jax.ShapeDtypeStruct((n, dout), out_dtype)]
    if emit_norm:
        out_specs.append(pl.BlockSpec((tm, d), lambda i, j: (i, 0)))
        out_shape.append(jax.ShapeDtypeStruct((n, d), F32))
    out = pl.pallas_call(
        _norm_matmul_kernel,
        grid=(n // tm, dout // tn),
        in_specs=[
            pl.BlockSpec((tm, d), lambda i, j: (i, 0)),
            pl.BlockSpec((1, d), lambda i, j: (0, 0)),
            pl.BlockSpec((d, tn), lambda i, j: (0, j)),
        ],
        out_specs=out_specs,
        out_shape=out_shape,
        scratch_shapes=[pltpu.VMEM((tm, d), BF16)],
        compiler_params=_params("parallel", "arbitrary"),
        name="norm_matmul",
    )(x, g.reshape(1, d), w)
    return tuple(out) if emit_norm else out[0]


def _gmlp_kernel(zu_ref, zv_ref, w_ref, b_ref, gv_ref, mix_ref, v_ref, *, chunk, n_chunks, gdim):
    tril = (lax.broadcasted_iota(jnp.int32, (chunk, chunk), 1)
            <= lax.broadcasted_iota(jnp.int32, (chunk, chunk), 0))
    for c in range(n_chunks):
        rows = slice(c * chunk, (c + 1) * chunk)
        u = jax.nn.gelu(zu_ref[rows, :])
        v = _rms(jax.nn.gelu(zv_ref[rows, :]), gv_ref[...])
        v_ref[rows, :] = v
        v16 = v.astype(BF16)
        for g in range(GMLP_GROUPS):
            cols = slice(g * gdim, (g + 1) * gdim)
            w = jnp.where(tril, w_ref[g], 0.0).astype(BF16)
            mixed = jnp.dot(w, v16[:, cols], preferred_element_type=F32) + b_ref[g]
            mix_ref[rows, cols] = (u[:, cols] * mixed).astype(BF16)


def gmlp_mix(z, w_s, b_s, g_v, *, chunk, width):
    n = z.shape[0]
    tm = min(512, n)
    assert n % tm == 0 and tm % chunk == 0
    gdim = width // GMLP_GROUPS
    w = w_s[:, :chunk, :chunk]
    b = b_s[:, :chunk].reshape(GMLP_GROUPS, chunk, 1)
    kern = functools.partial(_gmlp_kernel, chunk=chunk, n_chunks=tm // chunk, gdim=gdim)
    return pl.pallas_call(
        kern,
        grid=(n // tm,),
        in_specs=[
            pl.BlockSpec((tm, width), lambda i: (i, 0)),
            pl.BlockSpec((tm, width), lambda i: (i, 1)),
            pl.BlockSpec((GMLP_GROUPS, chunk, chunk), lambda i: (0, 0, 0)),
            pl.BlockSpec((GMLP_GROUPS, chunk, 1), lambda i: (0, 0, 0)),
            pl.BlockSpec((1, width), lambda i: (0, 0)),
        ],
        out_specs=[
            pl.BlockSpec((tm, width), lambda i: (i, 0)),
            pl.BlockSpec((tm, width), lambda i: (i, 0)),
        ],
        out_shape=[jax.ShapeDtypeStruct((n, width), BF16), jax.ShapeDtypeStruct((n, width), F32)],
        compiler_params=_params("parallel"),
        name="gmlp_mix",
    )(z, z, w, b, g_v.reshape(1, width))


def _mix_out_kernel(part_ref, qm_ref, mk_ref, mv_ref, wo_ref, x_ref, o_ref, *, n_sub, t_sub, width):
    scale = HEAD_DIM ** -0.5
    subs = []
    for bi in range(n_sub):
        rows = slice(bi * t_sub, (bi + 1) * t_sub)
        heads = []
        for h in range(MEM_HEADS):
            cols = slice(h * HEAD_DIM, (h + 1) * HEAD_DIM)
            q = qm_ref[rows, cols].astype(BF16)
            s = lax.dot_general(q, mk_ref[bi, :, cols], NT, preferred_element_type=F32) * scale
            s = s - jnp.max(s, axis=-1, keepdims=True)
            p = jnp.exp(s)
            p = (p / jnp.sum(p, axis=-1, keepdims=True)).astype(BF16)
            heads.append(jnp.dot(p, mv_ref[bi, :, cols], preferred_element_type=F32))
        subs.append(jnp.concatenate(heads, axis=1))
    mem = (subs[0] if n_sub == 1 else jnp.concatenate(subs, axis=0)).astype(BF16)
    y = jnp.dot(part_ref[...], wo_ref[:width, :], preferred_element_type=F32)
    y = y + jnp.dot(mem, wo_ref[width:, :], preferred_element_type=F32)
    o_ref[...] = x_ref[...] + y


def mix_out(part, z, q_block, mk, mv, w_o, x, *, seq):
    n, d = x.shape
    width = part.shape[1]
    n_mem = mk.shape[1]
    tm = min(256, n)
    assert n % tm == 0
    if seq >= tm:
        assert seq % tm == 0
        n_sub, t_sub = 1, tm
        kv_map = lambda i: ((i * tm) // seq, 0, 0)
    else:
        assert tm % seq == 0
        n_sub, t_sub = tm // seq, seq
        kv_map = lambda i: (i, 0, 0)
    kern = functools.partial(_mix_out_kernel, n_sub=n_sub, t_sub=t_sub, width=width)
    return pl.pallas_call(
        kern,
        grid=(n // tm,),
        in_specs=[
            pl.BlockSpec((tm, width), lambda i: (i, 0)),
            pl.BlockSpec((tm, MEM_WIDTH), lambda i: (i, q_block)),
            pl.BlockSpec((n_sub, n_mem, MEM_WIDTH), kv_map),
            pl.BlockSpec((n_sub, n_mem, MEM_WIDTH), kv_map),
            pl.BlockSpec((width + MEM_WIDTH, d), lambda i: (0, 0)),
            pl.BlockSpec((tm, d), lambda i: (i, 0)),
        ],
        out_specs=pl.BlockSpec((tm, d), lambda i: (i, 0)),
        out_shape=jax.ShapeDtypeStruct((n, d), F32),
        compiler_params=_params("parallel"),
        name="mix_out",
    )(part, z, mk, mv, w_o, x)


def _log_sigmoid(x):
    return jnp.minimum(x, 0.0) - jnp.log1p(jnp.exp(-jnp.abs(x)))


def _cumsum_kernel(*refs, has_prev, t_new):
    if has_prev:
        prev_ref, raw_ref, b_ref, c_ref, lf_ref = refs
    else:
        raw_ref, b_ref, c_ref, lf_ref = refs
    raw = raw_ref[0]
    lf = _log_sigmoid(raw + b_ref[...])
    lf = jnp.where(lax.broadcasted_iota(jnp.int32, lf.shape, 1) < t_new, lf, 0.0)
    lf_ref[0] = lf
    full = jnp.concatenate([prev_ref[0], lf], axis=1) if has_prev else lf
    total = full.shape[1]
    lane = lax.broadcasted_iota(jnp.int32, full.shape, 1)
    shift = 1
    while shift < total:
        full = full + jnp.where(lane >= shift, pltpu.roll(full, shift, 1), 0.0)
        shift *= 2
    c_ref[0] = full


def logf_cumsum(raw_t, b_col, t_new, prev_t=None):
    nb, hp, t_pad = raw_t.shape
    past = 0 if prev_t is None else prev_t.shape[2]
    kern = functools.partial(_cumsum_kernel, has_prev=prev_t is not None, t_new=t_new)
    in_specs = [pl.BlockSpec((1, hp, t_pad), lambda i: (i, 0, 0)), pl.BlockSpec((hp, 1), lambda i: (0, 0))]
    args = [raw_t, b_col]
    if prev_t is not None:
        in_specs = [pl.BlockSpec((1, hp, past), lambda i: (i, 0, 0))] + in_specs
        args = [prev_t] + args
    return pl.pallas_call(
        kern,
        grid=(nb,),
        in_specs=in_specs,
        out_specs=[
            pl.BlockSpec((1, hp, past + t_pad), lambda i: (i, 0, 0)),
            pl.BlockSpec((1, hp, t_pad), lambda i: (i, 0, 0)),
        ],
        out_shape=[jax.ShapeDtypeStruct((nb, hp, past + t_pad), F32), jax.ShapeDtypeStruct((nb, hp, t_pad), F32)],
        compiler_params=_params("parallel"),
        name="logf_cumsum",
    )(*args)


FOX_TILE = 2048
FOX_HEADS_PER_STEP = 2


def _fox_kernel(qi_ref, ki_ref, q_ref, k_ref, v_ref, ck_ref, cq_ref, o_ref, m_ref, l_ref, acc_ref, *, tile):
    step = pl.program_id(2)
    qi = qi_ref[step]
    ki = ki_ref[step]

    @pl.when(ki == 0)
    def _():
        m_ref[...] = jnp.full(m_ref.shape, NEG_INF, F32)
        l_ref[...] = jnp.zeros(l_ref.shape, F32)
        acc_ref[...] = jnp.zeros(acc_ref.shape, F32)

    def block(masked):
        for hh in range(FOX_HEADS_PER_STEP):
            cols = slice(hh * HEAD_DIM, (hh + 1) * HEAD_DIM)
            s = lax.dot_general(q_ref[:, cols], k_ref[:, cols], NT, preferred_element_type=F32) * (HEAD_DIM ** -0.5)
            s = s - (ck_ref[0, hh] - cq_ref[0, hh, :, 0:1])
            if masked:
                qpos = lax.broadcasted_iota(jnp.int32, (tile, tile), 0)
                kpos = lax.broadcasted_iota(jnp.int32, (tile, tile), 1)
                s = jnp.where(kpos <= qpos, s, NEG_INF)
            m_prev = m_ref[hh]
            m_new = jnp.maximum(m_prev, jnp.max(s, axis=-1, keepdims=True))
            alpha = jnp.exp(m_prev - m_new)
            p = jnp.exp(s - m_new)
            l_ref[hh] = alpha * l_ref[hh] + jnp.sum(p, axis=-1, keepdims=True)
            acc_ref[:, cols] = alpha * acc_ref[:, cols] + jnp.dot(p.astype(BF16), v_ref[:, cols],
                                                                  preferred_element_type=F32)
            m_ref[hh] = m_new

    @pl.when(ki < qi)
    def _():
        block(False)

    @pl.when(ki == qi)
    def _():
        block(True)
        for hh in range(FOX_HEADS_PER_STEP):
            cols = slice(hh * HEAD_DIM, (hh + 1) * HEAD_DIM)
            o_ref[:, cols] = (acc_ref[:, cols] / l_ref[hh]).astype(o_ref.dtype)


def fox_prompt(zq, k, v, c, *, batch, seq, heads):
    n = zq.shape[0]
    tile = min(FOX_TILE, seq)
    nt = seq // tile
    hs = FOX_HEADS_PER_STEP
    assert heads % hs == 0
    width = hs * HEAD_DIM
    pairs = [(qi, ki) for qi in range(nt) for ki in range(qi + 1)]
    qi_tab = jnp.asarray([p[0] for p in pairs], jnp.int32)
    ki_tab = jnp.asarray([p[1] for p in pairs], jnp.int32)
    q_map = lambda b, h, s, qt, kt: (b * nt + qt[s], h)
    kv_map = lambda b, h, s, qt, kt: (b * nt + kt[s], h)
    grid_spec = pltpu.PrefetchScalarGridSpec(
        num_scalar_prefetch=2,
        grid=(batch, heads // hs, len(pairs)),
        in_specs=[
            pl.BlockSpec((tile, width), q_map),
            pl.BlockSpec((tile, width), kv_map),
            pl.BlockSpec((tile, width), kv_map),
            pl.BlockSpec((1, hs, 1, tile), lambda b, h, s, qt, kt: (b, h, 0, kt[s])),
            pl.BlockSpec((1, hs, 1, tile), lambda b, h, s, qt, kt: (b, h, 0, qt[s])),
        ],
        out_specs=pl.BlockSpec((tile, width), q_map),
        scratch_shapes=[pltpu.VMEM((hs, tile, 1), F32), pltpu.VMEM((hs, tile, 1), F32),
                        pltpu.VMEM((tile, width), F32)],
    )
    return pl.pallas_call(
        functools.partial(_fox_kernel, tile=tile),
        grid_spec=grid_spec,
        out_shape=jax.ShapeDtypeStruct((n, heads * HEAD_DIM), BF16),
        compiler_params=_params("parallel", "parallel", "arbitrary"),
        name="fox_prompt",
    )(qi_tab, ki_tab, zq, k, v, c, c)


def _fox_sample_kernel(q_ref, kc_ref, vc_ref, kn_ref, vn_ref, c_ref, o_ref, *, past, t_new):
    scale = HEAD_DIM ** -0.5
    q = q_ref[...].astype(BF16)
    c = c_ref[0, 0]
    c_ref0 = c[:, past:past + 1]
    s_old = lax.dot_general(q, kc_ref[0].astype(BF16), NT, preferred_element_type=F32) * scale
    s_old = s_old - (c[:, :past] - c_ref0)
    s_new = lax.dot_general(q, kn_ref[...].astype(BF16), NT, preferred_element_type=F32) * scale
    s_new = s_new - (c[:, past:past + t_new] - c_ref0)
    causal = (lax.broadcasted_iota(jnp.int32, (t_new, t_new), 1)
              <= lax.broadcasted_iota(jnp.int32, (t_new, t_new), 0))
    s_new = jnp.where(causal, s_new, NEG_INF)
    m = jnp.maximum(jnp.max(s_old, axis=-1, keepdims=True), jnp.max(s_new, axis=-1, keepdims=True))
    p_old = jnp.exp(s_old - m)
    p_new = jnp.exp(s_new - m)
    denom = jnp.sum(p_old, axis=-1, keepdims=True) + jnp.sum(p_new, axis=-1, keepdims=True)
    acc = jnp.dot(p_old.astype(BF16), vc_ref[0].astype(BF16), preferred_element_type=F32)
    acc = acc + jnp.dot(p_new.astype(BF16), vn_ref[...].astype(BF16), preferred_element_type=F32)
    o_ref[...] = (acc / denom).astype(o_ref.dtype)


def fox_sample(zq, k_cache, v_cache, k_new, v_new, c, *, batch, t_new, heads):
    past = k_cache.shape[1]
    c_len = c.shape[3]
    kern = functools.partial(_fox_sample_kernel, past=past, t_new=t_new)
    return pl.pallas_call(
        kern,
        grid=(batch, heads),
        in_specs=[
            pl.BlockSpec((t_new, HEAD_DIM), lambda b, h: (b, h)),
            pl.BlockSpec((1, past, HEAD_DIM), lambda b, h: (b, 0, h)),
            pl.BlockSpec((1, past, HEAD_DIM), lambda b, h: (b, 0, h)),
            pl.BlockSpec((t_new, HEAD_DIM), lambda b, h: (b, h)),
            pl.BlockSpec((t_new, HEAD_DIM), lambda b, h: (b, h)),
            pl.BlockSpec((1, 1, 1, c_len), lambda b, h: (b, h, 0, 0)),
        ],
        out_specs=pl.BlockSpec((t_new, HEAD_DIM), lambda b, h: (b, h)),
        out_shape=jax.ShapeDtypeStruct((batch * t_new, heads * HEAD_DIM), BF16),
        compiler_params=_params("parallel", "parallel"),
        name="fox_sample",
    )(zq, k_cache, v_cache, k_new, v_new, c)


def _top_rows(vals, count, payload=None):
    n_rows = vals.shape[0]
    rows = lax.broadcasted_iota(jnp.int32, vals.shape, 0)
    out_v, out_i = [], []
    for _ in range(count):
        m = jnp.max(vals, axis=0, keepdims=True)
        idx = jnp.min(jnp.where(vals == m, rows, n_rows), axis=0, keepdims=True)
        hit = rows == idx
        out_v.append(m)
        out_i.append(idx if payload is None else jnp.max(jnp.where(hit, payload, -1), axis=0, keepdims=True))
        vals = jnp.where(hit, -jnp.inf, vals)
    return out_v, out_i


def _peer_route_kernel(q_ref, keys_ref, eidx_ref, gate_ref, e_scr, g_scr):
    h = pl.program_id(1)
    q = q_ref[...].astype(BF16)
    half_v, half_i = [], []
    for p in range(2):
        keys = keys_ref[0, p].astype(BF16)
        s = lax.dot_general(keys, q[:, p * LANES:(p + 1) * LANES], NT, preferred_element_type=F32)
        sv, si = _top_rows(s, PEER_TOPK)
        half_v.append(sv)
        half_i.append(si)
    v0 = jnp.concatenate(half_v[0], axis=0)
    i0 = jnp.concatenate(half_i[0], axis=0)
    v1 = jnp.concatenate(half_v[1], axis=0)
    i1 = jnp.concatenate(half_i[1], axis=0)
    half = PEER_TOPK // 2
    sub = lax.broadcasted_iota(jnp.int32, (half, v0.shape[1]), 0)
    cand = [half_v[0][0] + v1]
    cidx = [half_i[0][0] * PEER_NKEYS + i1]
    for a in range(1, half):
        keep = sub < PEER_TOPK // (a + 1)
        cand.append(jnp.where(keep, half_v[0][a] + v1[:half], -jnp.inf))
        cidx.append(half_i[0][a] * PEER_NKEYS + i1[:half])
    cand.append(v0[half:] + half_v[1][0])
    cidx.append(i0[half:] * PEER_NKEYS + half_i[1][0])
    best, eidx = _top_rows(jnp.concatenate(cand, axis=0), PEER_TOPK, payload=jnp.concatenate(cidx, axis=0))
    best = jnp.concatenate(best, axis=0)
    e = jnp.exp(best - jnp.max(best, axis=0, keepdims=True))
    gate = e / jnp.sum(e, axis=0, keepdims=True)
    row0 = pl.multiple_of(h * PEER_TOPK, PEER_TOPK)
    e_scr[pl.ds(row0, PEER_TOPK), :] = jnp.concatenate(eidx, axis=0)
    g_scr[pl.ds(row0, PEER_TOPK), :] = gate

    @pl.when(h == pl.num_programs(1) - 1)
    def _():
        eidx_ref[...] = e_scr[...].T
        gate_ref[...] = g_scr[...].T


def peer_route(q, sub_keys):
    n = q.shape[0]
    tt = min(256, n)
    assert n % tt == 0
    return pl.pallas_call(
        _peer_route_kernel,
        grid=(n // tt, PEER_HEADS),
        in_specs=[
            pl.BlockSpec((tt, 2 * LANES), lambda i, h: (i, h)),
            pl.BlockSpec((1, 2, PEER_NKEYS, LANES), lambda i, h: (h, 0, 0, 0)),
        ],
        out_specs=[
            pl.BlockSpec((tt, PEER_PAIRS), lambda i, h: (i, 0)),
            pl.BlockSpec((tt, PEER_PAIRS), lambda i, h: (i, 0)),
        ],
        out_shape=[jax.ShapeDtypeStruct((n, PEER_PAIRS), jnp.int32), jax.ShapeDtypeStruct((n, PEER_PAIRS), F32)],
        scratch_shapes=[pltpu.VMEM((PEER_PAIRS, tt), jnp.int32), pltpu.VMEM((PEER_PAIRS, tt), F32)],
        compiler_params=_params("parallel", "arbitrary"),
        name="peer_route",
    )(q, sub_keys)


PEER_SPLIT = 8
PEER_CHUNK = PEER_SPLIT * LANES
PEER_ROWS = PEER_PAIRS * PEER_SPLIT
PACKED_ROWS = PEER_SPLIT // 2


def _peer_pass_kernel(*refs, mode, group):
    if mode == "u_first":
        idx_ref, h_ref, slab_hbm, o_ref, slab, stage, sem = refs
    elif mode == "u_last":
        idx_ref, h_ref, w_ref, gate_ref, slab_hbm, o_ref, slab, stage, sem = refs
    else:
        idx_ref, coef_ref, x_ref, slab_hbm, o_ref, slab, stage, sem = refs

    @pl.when(pl.program_id(0) == 0)
    def _():
        load = pltpu.make_async_copy(slab_hbm, slab, sem.at[0])
        load.start()
        load.wait()

    lane = lax.broadcasted_iota(jnp.int32, (PEER_SPLIT, PEER_ROWS), 1)
    diag = (lane % PEER_SPLIT) == lax.broadcasted_iota(jnp.int32, (PEER_SPLIT, PEER_ROWS), 0)

    def one_token(t, j):
        for k in range(PEER_PAIRS):
            row = pl.multiple_of(idx_ref[t, k], PACKED_ROWS)
            stage[j, pl.ds(k * PACKED_ROWS, PACKED_ROWS), :] = slab[pl.ds(row, PACKED_ROWS), :]
        m16 = pltpu.bitcast(stage[j], BF16)
        if mode == "v":
            coef = coef_ref[pl.ds(t, 1), :]
            lhs = jnp.where(diag, jnp.broadcast_to(coef, (PEER_SPLIT, PEER_ROWS)), 0.0).astype(BF16)
            o_ref[t] = x_ref[t] + jnp.dot(lhs, m16, preferred_element_type=F32)
            return
        yt = lax.dot_general(h_ref[t].astype(BF16), m16, NT, preferred_element_type=F32)
        w = jnp.sum(jnp.where(diag, yt, 0.0), axis=0, keepdims=True)
        o_ref[pl.ds(t, 1), :] = w

    def body(i, carry):
        for j in range(PEER_UNROLL):
            one_token(i * PEER_UNROLL + j, j)
        return carry

    lax.fori_loop(0, group // PEER_UNROLL, body, 0)
    if mode == "u_last":
        w = o_ref[...] + w_ref[...]
        lane_g = lax.broadcasted_iota(jnp.int32, w.shape, 1)
        for s in (1, 2, 4):
            up = pltpu.roll(w, s, 1)
            down = pltpu.roll(w, PEER_ROWS - s, 1)
            w = w + jnp.where((lane_g & s) != 0, up, down)
        o_ref[...] = gate_ref[...] * jax.nn.gelu(w)


def peer_pass(mode, idx, slab, *operands, group=64):
    n = idx.shape[0]
    group = min(group, n)
    assert n % group == 0 and group % PEER_UNROLL == 0
    vec = pl.BlockSpec((group, PEER_ROWS), lambda i: (i, 0))
    idx_spec = pl.BlockSpec((group, PEER_PAIRS), lambda i: (i, 0), memory_space=pltpu.SMEM)
    any_spec = pl.BlockSpec(memory_space=pl.ANY)
    if mode == "v":
        coef, x3, chunk = operands
        tile = pl.BlockSpec((group, PEER_SPLIT, LANES), lambda i: (i, chunk, 0))
        in_specs, args = [idx_spec, vec, tile, any_spec], (idx, coef, x3, slab)
        out_spec, out_shape = tile, jax.ShapeDtypeStruct(x3.shape, F32)
        aliases = {2: 0}
    else:
        h3, chunk = operands[0], operands[-1]
        tile = pl.BlockSpec((group, PEER_SPLIT, LANES), lambda i: (i, chunk, 0))
        extra = list(operands[1:-1])
        in_specs, args = [idx_spec, tile] + [vec] * len(extra) + [any_spec], (idx, h3, *extra, slab)
        out_spec, out_shape = vec, jax.ShapeDtypeStruct((n, PEER_ROWS), F32)
        aliases = {}
    return pl.pallas_call(
        functools.partial(_peer_pass_kernel, mode=mode, group=group),
        grid=(n // group,),
        in_specs=in_specs,
        out_specs=out_spec,
        out_shape=out_shape,
        scratch_shapes=[
            pltpu.VMEM(slab.shape, jnp.uint32),
            pltpu.VMEM((PEER_UNROLL, PEER_PAIRS * PACKED_ROWS, LANES), jnp.uint32),
            pltpu.SemaphoreType.DMA((1,)),
        ],
        input_output_aliases=aliases,
        compiler_params=_params("arbitrary"),
        name="peer_" + mode,
    )(*args)


def peer_table(u, v):
    def slabs(t):
        e, d = t.shape
        out = []
        for c in range(d // PEER_CHUNK):
            m = t[:, c * PEER_CHUNK:(c + 1) * PEER_CHUNK].astype(BF16).reshape(e * PACKED_ROWS, 2, LANES)
            out.append(lax.bitcast_convert_type(jnp.swapaxes(m, 1, 2), jnp.uint32))
        return out
    return slabs(u), slabs(v)


def peer_apply(x, h, eidx, gate, tables):
    n, d = x.shape
    u_slabs, v_slabs = tables
    idx = eidx * PACKED_ROWS
    gate8 = jnp.repeat(gate, PEER_SPLIT, axis=1)
    h3 = h.reshape(n, d // LANES, LANES)
    w = None
    for c, slab in enumerate(u_slabs):
        if c < len(u_slabs) - 1:
            assert w is None
            w = peer_pass("u_first", idx, slab, h3, c)
        else:
            w = peer_pass("u_last", idx, slab, h3, w, gate8, c)
    y3 = x.reshape(n, d // LANES, LANES)
    for c, slab in enumerate(v_slabs):
        y3 = peer_pass("v", idx, slab, w, y3, c)
    return y3.reshape(n, d)


def _rms_kernel(x_ref, g_ref, o_ref):
    o_ref[...] = _rms(x_ref[...], g_ref[...])


def rms_norm(x, g):
    n, d = x.shape
    tm = min(512, n)
    return pl.pallas_call(
        _rms_kernel,
        grid=(n // tm,),
        in_specs=[pl.BlockSpec((tm, d), lambda i: (i, 0)), pl.BlockSpec((1, d), lambda i: (0, 0))],
        out_specs=pl.BlockSpec((tm, d), lambda i: (i, 0)),
        out_shape=jax.ShapeDtypeStruct((n, d), F32),
        compiler_params=_params("parallel"),
        name="rms_norm",
    )(x, g.reshape(1, d))


def peer_ffn(x, g, w_pq16, sub_keys, tables, g_final=None):
    q, h = norm_matmul(x, g, w_pq16, emit_norm=True)
    eidx, gate = peer_route(q, sub_keys)
    y = peer_apply(x, h, eidx, gate, tables)
    return y if g_final is None else rms_norm(y, g_final)


def _forget_inputs(zf, batch, t_new, t_pad):
    raw = zf[:, :FOX_HEADS_PAD].reshape(batch, t_new, FOX_HEADS_PAD)
    raw = jnp.swapaxes(raw, 1, 2)
    return jnp.pad(raw, ((0, 0), (0, 0), (0, t_pad - t_new)))


def kernel(x_prompt, x_sample, mem_prompt, cache_mem_k, cache_mem_v, cache_fox_k, cache_fox_v, cache_fox_logf, g_mix, g_ffn, w_o, g_mem, w_mem_kv, w_pq, peer_sub_keys, peer_u, peer_v, w_in_a, gmlp_ws, gmlp_b, gmlp_gv, w_in_b, g_kv, w_kv, b_f, g_final):
    batch, seq, d = x_prompt.shape
    dec_batch, dec_seq, _ = x_sample.shape
    depth = g_mix.shape[0]
    n_a = w_in_a.shape[0]
    n_mem = mem_prompt.shape[1]
    past = cache_fox_k.shape[1]
    heads = cache_fox_k.shape[2]
    fox_width = heads * HEAD_DIM
    gmlp_width = gmlp_gv.shape[1]
    assert gmlp_width == fox_width and d == fox_width + MEM_WIDTH

    w_o16 = w_o.astype(BF16)
    w_pq16 = w_pq.astype(BF16)
    w_in_a16 = w_in_a.astype(BF16)
    w_in_b16 = w_in_b.astype(BF16)
    w_mem16 = w_mem_kv.astype(BF16)
    w_k16 = w_kv[:, :fox_width].astype(BF16)
    w_v16 = w_kv[:, fox_width:2 * fox_width].astype(BF16)
    w_f16 = jnp.pad(w_kv[:, 2 * fox_width:], ((0, 0), (0, LANES - heads))).astype(BF16)
    b_col = jnp.pad(b_f, (0, FOX_HEADS_PAD - heads)).reshape(FOX_HEADS_PAD, 1)
    uv = [peer_table(peer_u[l], peer_v[l]) for l in range(depth)]

    def ffn(x, l, final=False):
        return peer_ffn(x, g_ffn[l], w_pq16[l], peer_sub_keys[l], uv[l], g_final if final else None)

    def shared_kv(x):
        k = norm_matmul(x, g_kv, w_k16)
        v = norm_matmul(x, g_kv, w_v16)
        zf = norm_matmul(x, g_kv, w_f16)
        return k, v, zf

    def run_group(x3, mem_k, mem_v, chunk, fox_fn):
        b, t, _ = x3.shape
        x = x3.reshape(b * t, d)
        gmlp_v = []
        extras = None
        for l in range(depth):
            if l < n_a:
                z = norm_matmul(x, g_mix[l], w_in_a16[l])
                part, v_rows = gmlp_mix(z, gmlp_ws[l], gmlp_b[l], gmlp_gv[l], chunk=chunk, width=gmlp_width)
                gmlp_v.append(v_rows.reshape(b, t, gmlp_width))
                x = mix_out(part, z, 2 * gmlp_width // MEM_WIDTH, mem_k[l], mem_v[l], w_o16[l], x, seq=t)
                x = ffn(x, l)
                if l == n_a - 1:
                    kvf = shared_kv(x)
            else:
                z = norm_matmul(x, g_mix[l], w_in_b16[l - n_a])
                part, extras = fox_fn(z, kvf, extras)
                x = mix_out(part, z, fox_width // MEM_WIDTH, mem_k[l], mem_v[l], w_o16[l], x, seq=t)
                x = ffn(x, l, final=(l == depth - 1))
        return x.reshape(b, t, d), gmlp_v, extras

    mem2d = mem_prompt.reshape(batch * n_mem, d)
    mem_k_p, mem_v_p = [], []
    for l in range(depth):
        zkv = norm_matmul(mem2d, g_mem[l], w_mem16[l])
        mem_k_p.append(zkv[:, :MEM_WIDTH].reshape(batch, n_mem, MEM_WIDTH))
        mem_v_p.append(zkv[:, MEM_WIDTH:].reshape(batch, n_mem, MEM_WIDTH))

    def fox_p(z, kvf, extras):
        if extras is None:
            k, v, zf = kvf
            c, lf = logf_cumsum(_forget_inputs(zf, batch, seq, seq), b_col, seq)
            extras = (k, v, lf, c[:, :heads].reshape(batch, heads, 1, seq), k.astype(BF16), v.astype(BF16))
        k, v, lf, c, k16, v16 = extras
        return fox_prompt(z.astype(BF16), k16, v16, c, batch=batch, seq=seq, heads=heads), extras

    y_p, _, ex_p = run_group(x_prompt, [a.astype(BF16) for a in mem_k_p], [a.astype(BF16) for a in mem_v_p],
                             min(seq, GMLP_CHUNK), fox_p)
    fox_k_p = ex_p[0].reshape(batch, seq, heads, HEAD_DIM)
    fox_v_p = ex_p[1].reshape(batch, seq, heads, HEAD_DIM)
    fox_logf_p = jnp.swapaxes(ex_p[2][:, :heads, :seq], 1, 2)
    new_mem_k = jnp.stack(mem_k_p).reshape(depth, batch, n_mem, MEM_HEADS, HEAD_DIM)
    new_mem_v = jnp.stack(mem_v_p).reshape(depth, batch, n_mem, MEM_HEADS, HEAD_DIM)

    t_pad = -(-dec_seq // LANES) * LANES
    prev_t = jnp.pad(jnp.swapaxes(cache_fox_logf, 1, 2), ((0, 0), (0, FOX_HEADS_PAD - heads), (0, 0)))
    k_cache = cache_fox_k.reshape(dec_batch, past, fox_width)
    v_cache = cache_fox_v.reshape(dec_batch, past, fox_width)

    def fox_s(z, kvf, extras):
        if extras is None:
            k, v, zf = kvf
            c, lf = logf_cumsum(_forget_inputs(zf, dec_batch, dec_seq, t_pad), b_col, dec_seq, prev_t)
            extras = (k, v, lf, c[:, :heads].reshape(dec_batch, heads, 1, past + t_pad))
        k, v, lf, c = extras
        return fox_sample(z, k_cache, v_cache, k, v, c, batch=dec_batch, t_new=dec_seq, heads=heads), extras

    mem_k_s = [cache_mem_k[l].reshape(dec_batch, n_mem, MEM_WIDTH).astype(BF16) for l in range(depth)]
    mem_v_s = [cache_mem_v[l].reshape(dec_batch, n_mem, MEM_WIDTH).astype(BF16) for l in range(depth)]
    y_s, gmlp_v_s, ex_s = run_group(x_sample, mem_k_s, mem_v_s, min(dec_seq, GMLP_CHUNK), fox_s)
    fox_k_s = ex_s[0].reshape(dec_batch, dec_seq, heads, HEAD_DIM)
    fox_v_s = ex_s[1].reshape(dec_batch, dec_seq, heads, HEAD_DIM)
    fox_logf_s = jnp.swapaxes(ex_s[2][:, :heads, :dec_seq], 1, 2)

    return (y_p, y_s, new_mem_k, new_mem_v, fox_k_p, fox_v_p, fox_logf_p,
            fox_k_s, fox_v_s, fox_logf_s, jnp.stack(gmlp_v_s))
```

```python
import functools

import jax
import jax.numpy as jnp
from jax import lax
from jax.experimental import pallas as pl
from jax.experimental.pallas import tpu as pltpu

EPS = 1e-6
NEG_INF = -1e30
LANES = 128
HEAD_DIM = 128
MEM_HEADS = 4
MEM_WIDTH = MEM_HEADS * HEAD_DIM
GMLP_GROUPS = 4
GMLP_CHUNK = 128
PEER_HEADS = 8
PEER_TOPK = 16
PEER_NKEYS = 128
PEER_PAIRS = PEER_HEADS * PEER_TOPK
PEER_UNROLL = 16
FOX_HEADS_PAD = 16
VMEM_LIMIT = 56 * 1024 * 1024
BF16 = jnp.bfloat16
F32 = jnp.float32
NT = (((1,), (1,)), ((), ()))


def _params(*sem):
    return pltpu.CompilerParams(dimension_semantics=sem, vmem_limit_bytes=VMEM_LIMIT)


def _rms(x, g):
    return x * lax.rsqrt(jnp.mean(x * x, axis=-1, keepdims=True) + EPS) * g


def _norm_matmul_kernel(x_ref, g_ref, w_ref, o_ref, *rest, emit_norm, emit_bf16):
    xn_ref = rest[-1]

    @pl.when(pl.program_id(1) == 0)
    def _():
        h = _rms(x_ref[...], g_ref[...])
        xn_ref[...] = h.astype(BF16)
        if emit_norm:
            rest[0][...] = h

    y = jnp.dot(xn_ref[...], w_ref[...], preferred_element_type=F32)
    o_ref[...] = y.astype(o_ref.dtype)
    if emit_bf16:
        rest[-2][...] = y.astype(BF16)


def norm_matmul(x, g, w, *, tn=512, out_dtype=F32, emit_norm=False, emit_bf16=False):
    n, d = x.shape
    dout = w.shape[1]
    tm = min(512, n)
    tn = min(tn, dout)
    assert n % tm == 0 and dout % tn == 0
    out_specs = [pl.BlockSpec((tm, tn), lambda i, j: (i, j))]
    out_shape = [jax.ShapeDtypeStruct((n, dout), out_dtype)]
    if emit_norm:
        out_specs.append(pl.BlockSpec((tm, d), lambda i, j: (i, 0)))
        out_shape.append(jax.ShapeDtypeStruct((n, d), F32))
    if emit_bf16:
        out_specs.append(pl.BlockSpec((tm, tn), lambda i, j: (i, j)))
        out_shape.append(jax.ShapeDtypeStruct((n, dout), BF16))
    out = pl.pallas_call(
        functools.partial(_norm_matmul_kernel, emit_norm=emit_norm, emit_bf16=emit_bf16),
        grid=(n // tm, dout // tn),
        in_specs=[
            pl.BlockSpec((tm, d), lambda i, j: (i, 0)),
            pl.BlockSpec((1, d), lambda i, j: (0, 0)),
            pl.BlockSpec((d, tn), lambda i, j: (0, j)),
        ],
        out_specs=out_specs,
        out_shape=out_shape,
        scratch_shapes=[pltpu.VMEM((tm, d), BF16)],
        compiler_params=_params("parallel", "arbitrary"),
        name="norm_matmul",
    )(x, g.reshape(1, d), w)
    return out[0] if len(out) == 1 else tuple(out)


def _gmlp_kernel(zu_ref, zv_ref, w_ref, b_ref, gv_ref, mix_ref, v_ref, *, chunk, n_chunks, gdim):
    tril = (lax.broadcasted_iota(jnp.int32, (chunk, chunk), 1)
            <= lax.broadcasted_iota(jnp.int32, (chunk, chunk), 0))
    for c in range(n_chunks):
        rows = slice(c * chunk, (c + 1) * chunk)
        u = jax.nn.gelu(zu_ref[rows, :])
        v = _rms(jax.nn.gelu(zv_ref[rows, :]), gv_ref[...])
        v_ref[rows, :] = v
        v16 = v.astype(BF16)
        for g in range(GMLP_GROUPS):
            cols = slice(g * gdim, (g + 1) * gdim)
            w = jnp.where(tril, w_ref[g], 0.0).astype(BF16)
            mixed = jnp.dot(w, v16[:, cols], preferred_element_type=F32) + b_ref[g]
            mix_ref[rows, cols] = (u[:, cols] * mixed).astype(BF16)


def gmlp_mix(z, w_s, b_s, g_v, *, chunk, width):
    n = z.shape[0]
    tm = min(512, n)
    assert n % tm == 0 and tm % chunk == 0
    gdim = width // GMLP_GROUPS
    w = w_s[:, :chunk, :chunk]
    b = b_s[:, :chunk].reshape(GMLP_GROUPS, chunk, 1)
    kern = functools.partial(_gmlp_kernel, chunk=chunk, n_chunks=tm // chunk, gdim=gdim)
    return pl.pallas_call(
        kern,
        grid=(n // tm,),
        in_specs=[
            pl.BlockSpec((tm, width), lambda i: (i, 0)),
            pl.BlockSpec((tm, width), lambda i: (i, 1)),
            pl.BlockSpec((GMLP_GROUPS, chunk, chunk), lambda i: (0, 0, 0)),
            pl.BlockSpec((GMLP_GROUPS, chunk, 1), lambda i: (0, 0, 0)),
            pl.BlockSpec((1, width), lambda i: (0, 0)),
        ],
        out_specs=[
            pl.BlockSpec((tm, width), lambda i: (i, 0)),
            pl.BlockSpec((tm, width), lambda i: (i, 0)),
        ],
        out_shape=[jax.ShapeDtypeStruct((n, width), BF16), jax.ShapeDtypeStruct((n, width), F32)],
        compiler_params=_params("parallel"),
        name="gmlp_mix",
    )(z, z, w, b, g_v.reshape(1, width))


def _mix_out_kernel(part_ref, qm_ref, mk_ref, mv_ref, wo_ref, x_ref, o_ref, *, n_sub, t_sub, width):
    scale = HEAD_DIM ** -0.5
    subs = []
    for bi in range(n_sub):
        rows = slice(bi * t_sub, (bi + 1) * t_sub)
        heads = []
        for h in range(MEM_HEADS):
            cols = slice(h * HEAD_DIM, (h + 1) * HEAD_DIM)
            q = qm_ref[rows, cols].astype(BF16)
            s = lax.dot_general(q, mk_ref[bi, :, cols], NT, preferred_element_type=F32) * scale
            s = s - jnp.max(s, axis=-1, keepdims=True)
            p = jnp.exp(s)
            p = (p / jnp.sum(p, axis=-1, keepdims=True)).astype(BF16)
            heads.append(jnp.dot(p, mv_ref[bi, :, cols], preferred_element_type=F32))
        subs.append(jnp.concatenate(heads, axis=1))
    mem = (subs[0] if n_sub == 1 else jnp.concatenate(subs, axis=0)).astype(BF16)
    y = jnp.dot(part_ref[...], wo_ref[:width, :], preferred_element_type=F32)
    y = y + jnp.dot(mem, wo_ref[width:, :], preferred_element_type=F32)
    o_ref[...] = x_ref[...] + y


def mix_out(part, z, q_block, mk, mv, w_o, x, *, seq):
    n, d = x.shape
    width = part.shape[1]
    n_mem = mk.shape[1]
    tm = min(256, n)
    assert n % tm == 0
    if seq >= tm:
        assert seq % tm == 0
        n_sub, t_sub = 1, tm
        kv_map = lambda i: ((i * tm) // seq, 0, 0)
    else:
        assert tm % seq == 0
        n_sub, t_sub = tm // seq, seq
        kv_map = lambda i: (i, 0, 0)
    kern = functools.partial(_mix_out_kernel, n_sub=n_sub, t_sub=t_sub, width=width)
    return pl.pallas_call(
        kern,
        grid=(n // tm,),
        in_specs=[
            pl.BlockSpec((tm, width), lambda i: (i, 0)),
            pl.BlockSpec((tm, MEM_WIDTH), lambda i: (i, q_block)),
            pl.BlockSpec((n_sub, n_mem, MEM_WIDTH), kv_map),
            pl.BlockSpec((n_sub, n_mem, MEM_WIDTH), kv_map),
            pl.BlockSpec((width + MEM_WIDTH, d), lambda i: (0, 0)),
            pl.BlockSpec((tm, d), lambda i: (i, 0)),
        ],
        out_specs=pl.BlockSpec((tm, d), lambda i: (i, 0)),
        out_shape=jax.ShapeDtypeStruct((n, d), F32),
        compiler_params=_params("parallel"),
        name="mix_out",
    )(part, z, mk, mv, w_o, x)


def _log_sigmoid(x):
    return jnp.minimum(x, 0.0) - jnp.log1p(jnp.exp(-jnp.abs(x)))


def _cumsum_kernel(*refs, has_prev, t_new):
    if has_prev:
        prev_ref, raw_ref, b_ref, c_ref, lf_ref = refs
    else:
        raw_ref, b_ref, c_ref, lf_ref = refs
    raw = raw_ref[0]
    lf = _log_sigmoid(raw + b_ref[...])
    lf = jnp.where(lax.broadcasted_iota(jnp.int32, lf.shape, 1) < t_new, lf, 0.0)
    lf_ref[0] = lf
    full = jnp.concatenate([prev_ref[0], lf], axis=1) if has_prev else lf
    total = full.shape[1]
    lane = lax.broadcasted_iota(jnp.int32, full.shape, 1)
    shift = 1
    while shift < total:
        full = full + jnp.where(lane >= shift, pltpu.roll(full, shift, 1), 0.0)
        shift *= 2
    c_ref[0] = full


def logf_cumsum(raw_t, b_col, t_new, prev_t=None):
    nb, hp, t_pad = raw_t.shape
    past = 0 if prev_t is None else prev_t.shape[2]
    kern = functools.partial(_cumsum_kernel, has_prev=prev_t is not None, t_new=t_new)
    in_specs = [pl.BlockSpec((1, hp, t_pad), lambda i: (i, 0, 0)), pl.BlockSpec((hp, 1), lambda i: (0, 0))]
    args = [raw_t, b_col]
    if prev_t is not None:
        in_specs = [pl.BlockSpec((1, hp, past), lambda i: (i, 0, 0))] + in_specs
        args = [prev_t] + args
    return pl.pallas_call(
        kern,
        grid=(nb,),
        in_specs=in_specs,
        out_specs=[
            pl.BlockSpec((1, hp, past + t_pad), lambda i: (i, 0, 0)),
            pl.BlockSpec((1, hp, t_pad), lambda i: (i, 0, 0)),
        ],
        out_shape=[jax.ShapeDtypeStruct((nb, hp, past + t_pad), F32), jax.ShapeDtypeStruct((nb, hp, t_pad), F32)],
        compiler_params=_params("parallel"),
        name="logf_cumsum",
    )(*args)


FOX_TILE = 2048
FOX_HEADS_PER_STEP = 2


def _fox_kernel(qi_ref, ki_ref, q_ref, k_ref, v_ref, ck_ref, cq_ref, o_ref, m_ref, l_ref, acc_ref, *, tile):
    step = pl.program_id(2)
    qi = qi_ref[step]
    ki = ki_ref[step]

    @pl.when(ki == 0)
    def _():
        m_ref[...] = jnp.full(m_ref.shape, NEG_INF, F32)
        l_ref[...] = jnp.zeros(l_ref.shape, F32)
        acc_ref[...] = jnp.zeros(acc_ref.shape, F32)

    def block(masked):
        for hh in range(FOX_HEADS_PER_STEP):
            cols = slice(hh * HEAD_DIM, (hh + 1) * HEAD_DIM)
            s = lax.dot_general(q_ref[:, cols], k_ref[:, cols], NT, preferred_element_type=F32) * (HEAD_DIM ** -0.5)
            s = s - (ck_ref[0, hh] - cq_ref[0, hh, :, 0:1])
            if masked:
                qpos = lax.broadcasted_iota(jnp.int32, (tile, tile), 0)
                kpos = lax.broadcasted_iota(jnp.int32, (tile, tile), 1)
                s = jnp.where(kpos <= qpos, s, NEG_INF)
            m_prev = m_ref[hh]
            m_new = jnp.maximum(m_prev, jnp.max(s, axis=-1, keepdims=True))
            alpha = jnp.exp(m_prev - m_new)
            p = jnp.exp(s - m_new)
            l_ref[hh] = alpha * l_ref[hh] + jnp.sum(p, axis=-1, keepdims=True)
            acc_ref[:, cols] = alpha * acc_ref[:, cols] + jnp.dot(p.astype(BF16), v_ref[:, cols],
                                                                  preferred_element_type=F32)
            m_ref[hh] = m_new

    @pl.when(ki < qi)
    def _():
        block(False)

    @pl.when(ki == qi)
    def _():
        block(True)
        for hh in range(FOX_HEADS_PER_STEP):
            cols = slice(hh * HEAD_DIM, (hh + 1) * HEAD_DIM)
            o_ref[:, cols] = (acc_ref[:, cols] / l_ref[hh]).astype(o_ref.dtype)


def fox_prompt(zq, k, v, c, *, batch, seq, heads):
    n = zq.shape[0]
    tile = min(FOX_TILE, seq)
    nt = seq // tile
    hs = FOX_HEADS_PER_STEP
    assert heads % hs == 0
    width = hs * HEAD_DIM
    pairs = [(qi, ki) for qi in range(nt) for ki in range(qi + 1)]
    qi_tab = jnp.asarray([p[0] for p in pairs], jnp.int32)
    ki_tab = jnp.asarray([p[1] for p in pairs], jnp.int32)
    q_map = lambda b, h, s, qt, kt: (b * nt + qt[s], h)
    kv_map = lambda b, h, s, qt, kt: (b * nt + kt[s], h)
    grid_spec = pltpu.PrefetchScalarGridSpec(
        num_scalar_prefetch=2,
        grid=(batch, heads // hs, len(pairs)),
        in_specs=[
            pl.BlockSpec((tile, width), q_map),
            pl.BlockSpec((tile, width), kv_map),
            pl.BlockSpec((tile, width), kv_map),
            pl.BlockSpec((1, hs, 1, tile), lambda b, h, s, qt, kt: (b, h, 0, kt[s])),
            pl.BlockSpec((1, hs, 1, tile), lambda b, h, s, qt, kt: (b, h, 0, qt[s])),
        ],
        out_specs=pl.BlockSpec((tile, width), q_map),
        scratch_shapes=[pltpu.VMEM((hs, tile, 1), F32), pltpu.VMEM((hs, tile, 1), F32),
                        pltpu.VMEM((tile, width), F32)],
    )
    return pl.pallas_call(
        functools.partial(_fox_kernel, tile=tile),
        grid_spec=grid_spec,
        out_shape=jax.ShapeDtypeStruct((n, heads * HEAD_DIM), BF16),
        compiler_params=_params("parallel", "parallel", "arbitrary"),
        name="fox_prompt",
    )(qi_tab, ki_tab, zq, k, v, c, c)


def _fox_sample_kernel(q_ref, kc_ref, vc_ref, kn_ref, vn_ref, c_ref, o_ref, *, past, t_new):
    scale = HEAD_DIM ** -0.5
    q = q_ref[...].astype(BF16)
    c = c_ref[0, 0]
    c_ref0 = c[:, past:past + 1]
    s_old = lax.dot_general(q, kc_ref[0].astype(BF16), NT, preferred_element_type=F32) * scale
    s_old = s_old - (c[:, :past] - c_ref0)
    s_new = lax.dot_general(q, kn_ref[...].astype(BF16), NT, preferred_element_type=F32) * scale
    s_new = s_new - (c[:, past:past + t_new] - c_ref0)
    causal = (lax.broadcasted_iota(jnp.int32, (t_new, t_new), 1)
              <= lax.broadcasted_iota(jnp.int32, (t_new, t_new), 0))
    s_new = jnp.where(causal, s_new, NEG_INF)
    m = jnp.maximum(jnp.max(s_old, axis=-1, keepdims=True), jnp.max(s_new, axis=-1, keepdims=True))
    p_old = jnp.exp(s_old - m)
    p_new = jnp.exp(s_new - m)
    denom = jnp.sum(p_old, axis=-1, keepdims=True) + jnp.sum(p_new, axis=-1, keepdims=True)
    acc = jnp.dot(p_old.astype(BF16), vc_ref[0].astype(BF16), preferred_element_type=F32)
    acc = acc + jnp.dot(p_new.astype(BF16), vn_ref[...].astype(BF16), preferred_element_type=F32)
    o_ref[...] = (acc / denom).astype(o_ref.dtype)


def fox_sample(zq, k_cache, v_cache, k_new, v_new, c, *, batch, t_new, heads):
    past = k_cache.shape[1]
    c_len = c.shape[3]
    kern = functools.partial(_fox_sample_kernel, past=past, t_new=t_new)
    return pl.pallas_call(
        kern,
        grid=(batch, heads),
        in_specs=[
            pl.BlockSpec((t_new, HEAD_DIM), lambda b, h: (b, h)),
            pl.BlockSpec((1, past, HEAD_DIM), lambda b, h: (b, 0, h)),
            pl.BlockSpec((1, past, HEAD_DIM), lambda b, h: (b, 0, h)),
            pl.BlockSpec((t_new, HEAD_DIM), lambda b, h: (b, h)),
            pl.BlockSpec((t_new, HEAD_DIM), lambda b, h: (b, h)),
            pl.BlockSpec((1, 1, 1, c_len), lambda b, h: (b, h, 0, 0)),
        ],
        out_specs=pl.BlockSpec((t_new, HEAD_DIM), lambda b, h: (b, h)),
        out_shape=jax.ShapeDtypeStruct((batch * t_new, heads * HEAD_DIM), BF16),
        compiler_params=_params("parallel", "parallel"),
        name="fox_sample",
    )(zq, k_cache, v_cache, k_new, v_new, c)


def _top_rows(vals, count, payload=None):
    n_rows = vals.shape[0]
    rows = lax.broadcasted_iota(jnp.int32, vals.shape, 0)
    out_v, out_i = [], []
    for _ in range(count):
        m = jnp.max(vals, axis=0, keepdims=True)
        idx = jnp.min(jnp.where(vals == m, rows, n_rows), axis=0, keepdims=True)
        hit = rows == idx
        out_v.append(m)
        out_i.append(idx if payload is None else jnp.max(jnp.where(hit, payload, -1), axis=0, keepdims=True))
        vals = jnp.where(hit, -jnp.inf, vals)
    return out_v, out_i


def _peer_route_kernel(q_ref, keys_ref, eidx_ref, gate_ref, e_scr, g_scr):
    h = pl.program_id(1)
    q = q_ref[...].astype(BF16)
    half_v, half_i = [], []
    for p in range(2):
        keys = keys_ref[0, p].astype(BF16)
        s = lax.dot_general(keys, q[:, p * LANES:(p + 1) * LANES], NT, preferred_element_type=F32)
        sv, si = _top_rows(s, PEER_TOPK)
        half_v.append(sv)
        half_i.append(si)
    v0 = jnp.concatenate(half_v[0], axis=0)
    i0 = jnp.concatenate(half_i[0], axis=0)
    v1 = jnp.concatenate(half_v[1], axis=0)
    i1 = jnp.concatenate(half_i[1], axis=0)
    half = PEER_TOPK // 2
    sub = lax.broadcasted_iota(jnp.int32, (half, v0.shape[1]), 0)
    cand = [half_v[0][0] + v1]
    cidx = [half_i[0][0] * PEER_NKEYS + i1]
    for a in range(1, half):
        keep = sub < PEER_TOPK // (a + 1)
        cand.append(jnp.where(keep, half_v[0][a] + v1[:half], -jnp.inf))
        cidx.append(half_i[0][a] * PEER_NKEYS + i1[:half])
    cand.append(v0[half:] + half_v[1][0])
    cidx.append(i0[half:] * PEER_NKEYS + half_i[1][0])
    best, eidx = _top_rows(jnp.concatenate(cand, axis=0), PEER_TOPK, payload=jnp.concatenate(cidx, axis=0))
    best = jnp.concatenate(best, axis=0)
    e = jnp.exp(best - jnp.max(best, axis=0, keepdims=True))
    gate = e / jnp.sum(e, axis=0, keepdims=True)
    row0 = pl.multiple_of(h * PEER_TOPK, PEER_TOPK)
    e_scr[pl.ds(row0, PEER_TOPK), :] = jnp.concatenate(eidx, axis=0)
    g_scr[pl.ds(row0, PEER_TOPK), :] = gate

    @pl.when(h == pl.num_programs(1) - 1)
    def _():
        eidx_ref[...] = e_scr[...].T
        gate_ref[...] = g_scr[...].T


def peer_route(q, sub_keys):
    n = q.shape[0]
    tt = min(256, n)
    assert n % tt == 0
    return pl.pallas_call(
        _peer_route_kernel,
        grid=(n // tt, PEER_HEADS),
        in_specs=[
            pl.BlockSpec((tt, 2 * LANES), lambda i, h: (i, h)),
            pl.BlockSpec((1, 2, PEER_NKEYS, LANES), lambda i, h: (h, 0, 0, 0)),
        ],
        out_specs=[
            pl.BlockSpec((tt, PEER_PAIRS), lambda i, h: (i, 0)),
            pl.BlockSpec((tt, PEER_PAIRS), lambda i, h: (i, 0)),
        ],
        out_shape=[jax.ShapeDtypeStruct((n, PEER_PAIRS), jnp.int32), jax.ShapeDtypeStruct((n, PEER_PAIRS), F32)],
        scratch_shapes=[pltpu.VMEM((PEER_PAIRS, tt), jnp.int32), pltpu.VMEM((PEER_PAIRS, tt), F32)],
        compiler_params=_params("parallel", "arbitrary"),
        name="peer_route",
    )(q, sub_keys)


PEER_SPLIT = 8
PEER_CHUNK = PEER_SPLIT * LANES
PEER_ROWS = PEER_PAIRS * PEER_SPLIT
PACKED_ROWS = PEER_SPLIT // 2


def _peer_pass_kernel(*refs, mode, group):
    if mode == "u_first":
        idx_ref, h_ref, slab_hbm, o_ref, slab, stage, tiles, sem = refs
    elif mode == "u_last":
        idx_ref, h_ref, w_ref, gate_ref, slab_hbm, o_ref, slab, stage, tiles, sem = refs
    else:
        idx_ref, coef_ref, x_ref, slab_hbm, o_ref, slab, stage, tiles, sem = refs

    @pl.when(pl.program_id(0) == 0)
    def _():
        load = pltpu.make_async_copy(slab_hbm, slab, sem.at[0])
        load.start()
        load.wait()

    if mode != "v":
        for r in range(PEER_SPLIT):
            tiles[pl.ds(r, group, stride=PEER_SPLIT), :] = h_ref[:, r * LANES:(r + 1) * LANES]

    lane = lax.broadcasted_iota(jnp.int32, (PEER_SPLIT, PEER_ROWS), 1)
    diag = (lane % PEER_SPLIT) == lax.broadcasted_iota(jnp.int32, (PEER_SPLIT, PEER_ROWS), 0)

    def one_token(t, j):
        for k in range(PEER_PAIRS):
            row = pl.multiple_of(idx_ref[t, k], PACKED_ROWS)
            stage[j, pl.ds(k * PACKED_ROWS, PACKED_ROWS), :] = slab[pl.ds(row, PACKED_ROWS), :]
        m16 = pltpu.bitcast(stage[j], BF16)
        r0 = pl.multiple_of(t * PEER_SPLIT, PEER_SPLIT)
        if mode == "v":
            coef = coef_ref[pl.ds(t, 1), :]
            lhs = jnp.where(diag, jnp.broadcast_to(coef, (PEER_SPLIT, PEER_ROWS)), 0.0).astype(BF16)
            tiles[pl.ds(r0, PEER_SPLIT), :] = jnp.dot(lhs, m16, preferred_element_type=F32)
            return
        h16 = tiles[pl.ds(r0, PEER_SPLIT), :].astype(BF16)
        yt = lax.dot_general(h16, m16, NT, preferred_element_type=F32)
        o_ref[pl.ds(t, 1), :] = jnp.sum(jnp.where(diag, yt, 0.0), axis=0, keepdims=True)

    def body(i, carry):
        for j in range(PEER_UNROLL):
            one_token(i * PEER_UNROLL + j, j)
        return carry

    lax.fori_loop(0, group // PEER_UNROLL, body, 0)
    if mode == "v":
        for r in range(PEER_SPLIT):
            cols = slice(r * LANES, (r + 1) * LANES)
            o_ref[:, cols] = x_ref[:, cols] + tiles[pl.ds(r, group, stride=PEER_SPLIT), :]
    if mode == "u_last":
        w = o_ref[...] + w_ref[...]
        lane_g = lax.broadcasted_iota(jnp.int32, w.shape, 1)
        for s in (1, 2, 4):
            up = pltpu.roll(w, s, 1)
            down = pltpu.roll(w, PEER_ROWS - s, 1)
            w = w + jnp.where((lane_g & s) != 0, up, down)
        o_ref[...] = gate_ref[...] * jax.nn.gelu(w)


def peer_pass(mode, idx, slab, *operands, group=64):
    n = idx.shape[0]
    group = min(group, n)
    assert n % group == 0 and group % PEER_UNROLL == 0
    vec = pl.BlockSpec((group, PEER_ROWS), lambda i: (i, 0))
    idx_spec = pl.BlockSpec((group, PEER_PAIRS), lambda i: (i, 0), memory_space=pltpu.SMEM)
    any_spec = pl.BlockSpec(memory_space=pl.ANY)
    chunk = operands[-1]
    cols = pl.BlockSpec((group, PEER_CHUNK), lambda i: (i, chunk))
    if mode == "v":
        coef, x = operands[:2]
        in_specs, args = [idx_spec, vec, cols, any_spec], (idx, coef, x, slab)
        out_spec, out_shape = cols, jax.ShapeDtypeStruct(x.shape, F32)
        aliases = {2: 0}
    else:
        h = operands[0]
        extra = list(operands[1:-1])
        in_specs, args = [idx_spec, cols] + [vec] * len(extra) + [any_spec], (idx, h, *extra, slab)
        out_spec, out_shape = vec, jax.ShapeDtypeStruct((n, PEER_ROWS), F32)
        aliases = {}
    return pl.pallas_call(
        functools.partial(_peer_pass_kernel, mode=mode, group=group),
        grid=(n // group,),
        in_specs=in_specs,
        out_specs=out_spec,
        out_shape=out_shape,
        scratch_shapes=[
            pltpu.VMEM(slab.shape, jnp.uint32),
            pltpu.VMEM((PEER_UNROLL, PEER_PAIRS * PACKED_ROWS, LANES), jnp.uint32),
            pltpu.VMEM((group * PEER_SPLIT, LANES), F32),
            pltpu.SemaphoreType.DMA((1,)),
        ],
        input_output_aliases=aliases,
        compiler_params=_params("arbitrary"),
        name="peer_" + mode,
    )(*args)


def peer_table(u, v):
    def slabs(t):
        e, d = t.shape
        bits = lax.bitcast_convert_type(t.astype(BF16), jnp.uint16).astype(jnp.uint32)
        bits = bits.reshape(e, d // PEER_CHUNK, PACKED_ROWS, 2, LANES)
        packed = bits[:, :, :, 0, :] | (bits[:, :, :, 1, :] << 16)
        return [packed[:, c].reshape(e * PACKED_ROWS, LANES) for c in range(d // PEER_CHUNK)]
    return slabs(u), slabs(v)


def peer_apply(x, h, eidx, gate, tables):
    u_slabs, v_slabs = tables
    idx = eidx * PACKED_ROWS
    gate8 = jnp.repeat(gate, PEER_SPLIT, axis=1)
    w = None
    for c, slab in enumerate(u_slabs):
        if c < len(u_slabs) - 1:
            assert w is None
            w = peer_pass("u_first", idx, slab, h, c)
        else:
            w = peer_pass("u_last", idx, slab, h, w, gate8, c)
    y = x
    for c, slab in enumerate(v_slabs):
        y = peer_pass("v", idx, slab, w, y, c)
    return y


def _rms_kernel(x_ref, g_ref, o_ref):
    o_ref[...] = _rms(x_ref[...], g_ref[...])


def rms_norm(x, g):
    n, d = x.shape
    tm = min(512, n)
    return pl.pallas_call(
        _rms_kernel,
        grid=(n // tm,),
        in_specs=[pl.BlockSpec((tm, d), lambda i: (i, 0)), pl.BlockSpec((1, d), lambda i: (0, 0))],
        out_specs=pl.BlockSpec((tm, d), lambda i: (i, 0)),
        out_shape=jax.ShapeDtypeStruct((n, d), F32),
        compiler_params=_params("parallel"),
        name="rms_norm",
    )(x, g.reshape(1, d))


def peer_ffn(x, g, w_pq16, sub_keys, tables, g_final=None):
    q, h = norm_matmul(x, g, w_pq16, emit_norm=True)
    eidx, gate = peer_route(q, sub_keys)
    y = peer_apply(x, h, eidx, gate, tables)
    return y if g_final is None else rms_norm(y, g_final)


def _forget_inputs(zf, batch, t_new, t_pad):
    raw = zf[:, :FOX_HEADS_PAD].reshape(batch, t_new, FOX_HEADS_PAD)
    raw = jnp.swapaxes(raw, 1, 2)
    return jnp.pad(raw, ((0, 0), (0, 0), (0, t_pad - t_new)))


def kernel(x_prompt, x_sample, mem_prompt, cache_mem_k, cache_mem_v, cache_fox_k, cache_fox_v, cache_fox_logf, g_mix, g_ffn, w_o, g_mem, w_mem_kv, w_pq, peer_sub_keys, peer_u, peer_v, w_in_a, gmlp_ws, gmlp_b, gmlp_gv, w_in_b, g_kv, w_kv, b_f, g_final):
    batch, seq, d = x_prompt.shape
    dec_batch, dec_seq, _ = x_sample.shape
    depth = g_mix.shape[0]
    n_a = w_in_a.shape[0]
    n_mem = mem_prompt.shape[1]
    past = cache_fox_k.shape[1]
    heads = cache_fox_k.shape[2]
    fox_width = heads * HEAD_DIM
    gmlp_width = gmlp_gv.shape[1]
    assert gmlp_width == fox_width and d == fox_width + MEM_WIDTH

    w_o16 = w_o.astype(BF16)
    w_pq16 = w_pq.astype(BF16)
    w_in_a16 = w_in_a.astype(BF16)
    w_in_b16 = w_in_b.astype(BF16)
    w_mem16 = w_mem_kv.astype(BF16)
    w_k16 = w_kv[:, :fox_width].astype(BF16)
    w_v16 = w_kv[:, fox_width:2 * fox_width].astype(BF16)
    w_f16 = jnp.pad(w_kv[:, 2 * fox_width:], ((0, 0), (0, LANES - heads))).astype(BF16)
    b_col = jnp.pad(b_f, (0, FOX_HEADS_PAD - heads)).reshape(FOX_HEADS_PAD, 1)
    uv = [peer_table(peer_u[l], peer_v[l]) for l in range(depth)]

    def ffn(x, l, final=False):
        return peer_ffn(x, g_ffn[l], w_pq16[l], peer_sub_keys[l], uv[l], g_final if final else None)

    def shared_kv(x):
        k, k16 = norm_matmul(x, g_kv, w_k16, emit_bf16=True)
        v, v16 = norm_matmul(x, g_kv, w_v16, emit_bf16=True)
        zf = norm_matmul(x, g_kv, w_f16)
        return k, v, zf, k16, v16

    def run_group(x3, mem_k, mem_v, chunk, fox_fn):
        b, t, _ = x3.shape
        x = x3.reshape(b * t, d)
        gmlp_v = []
        extras = None
        for l in range(depth):
            if l < n_a:
                z = norm_matmul(x, g_mix[l], w_in_a16[l])
                part, v_rows = gmlp_mix(z, gmlp_ws[l], gmlp_b[l], gmlp_gv[l], chunk=chunk, width=gmlp_width)
                gmlp_v.append(v_rows.reshape(b, t, gmlp_width))
                x = mix_out(part, z, 2 * gmlp_width // MEM_WIDTH, mem_k[l], mem_v[l], w_o16[l], x, seq=t)
                x = ffn(x, l)
                if l == n_a - 1:
                    kvf = shared_kv(x)
            else:
                z = norm_matmul(x, g_mix[l], w_in_b16[l - n_a], out_dtype=BF16)
                part, extras = fox_fn(z, kvf, extras)
                x = mix_out(part, z, fox_width // MEM_WIDTH, mem_k[l], mem_v[l], w_o16[l], x, seq=t)
                x = ffn(x, l, final=(l == depth - 1))
        return x.reshape(b, t, d), gmlp_v, extras

    mem2d = mem_prompt.reshape(batch * n_mem, d)
    mem_k_p, mem_v_p = [], []
    for l in range(depth):
        zkv = norm_matmul(mem2d, g_mem[l], w_mem16[l])
        mem_k_p.append(zkv[:, :MEM_WIDTH].reshape(batch, n_mem, MEM_WIDTH))
        mem_v_p.append(zkv[:, MEM_WIDTH:].reshape(batch, n_mem, MEM_WIDTH))

    def fox_p(z, kvf, extras):
        if extras is None:
            k, v, zf, k16, v16 = kvf
            c, lf = logf_cumsum(_forget_inputs(zf, batch, seq, seq), b_col, seq)
            extras = (k, v, lf, c[:, :heads].reshape(batch, heads, 1, seq), k16, v16)
        k, v, lf, c, k16, v16 = extras
        return fox_prompt(z, k16, v16, c, batch=batch, seq=seq, heads=heads), extras

    y_p, _, ex_p = run_group(x_prompt, [a.astype(BF16) for a in mem_k_p], [a.astype(BF16) for a in mem_v_p],
                             min(seq, GMLP_CHUNK), fox_p)
    fox_k_p = ex_p[0].reshape(batch, seq, heads, HEAD_DIM)
    fox_v_p = ex_p[1].reshape(batch, seq, heads, HEAD_DIM)
    fox_logf_p = jnp.swapaxes(ex_p[2][:, :heads, :seq], 1, 2)
    new_mem_k = jnp.stack(mem_k_p).reshape(depth, batch, n_mem, MEM_HEADS, HEAD_DIM)
    new_mem_v = jnp.stack(mem_v_p).reshape(depth, batch, n_mem, MEM_HEADS, HEAD_DIM)

    t_pad = -(-dec_seq // LANES) * LANES
    prev_t = jnp.pad(jnp.swapaxes(cache_fox_logf, 1, 2), ((0, 0), (0, FOX_HEADS_PAD - heads), (0, 0)))
    k_cache = cache_fox_k.reshape(dec_batch, past, fox_width)
    v_cache = cache_fox_v.reshape(dec_batch, past, fox_width)

    def fox_s(z, kvf, extras):
        if extras is None:
            k, v, zf = kvf[:3]
            c, lf = logf_cumsum(_forget_inputs(zf, dec_batch, dec_seq, t_pad), b_col, dec_seq, prev_t)
            extras = (k, v, lf, c[:, :heads].reshape(dec_batch, heads, 1, past + t_pad))
        k, v, lf, c = extras
        return fox_sample(z, k_cache, v_cache, k, v, c, batch=dec_batch, t_new=dec_seq, heads=heads), extras

    mem_k_s = [cache_mem_k[l].reshape(dec_batch, n_mem, MEM_WIDTH).astype(BF16) for l in range(depth)]
    mem_v_s = [cache_mem_v[l].reshape(dec_batch, n_mem, MEM_WIDTH).astype(BF16) for l in range(depth)]
    y_s, gmlp_v_s, ex_s = run_group(x_sample, mem_k_s, mem_v_s, min(dec_seq, GMLP_CHUNK), fox_s)
    fox_k_s = ex_s[0].reshape(dec_batch, dec_seq, heads, HEAD_DIM)
    fox_v_s = ex_s[1].reshape(dec_batch, dec_seq, heads, HEAD_DIM)
    fox_logf_s = jnp.swapaxes(ex_s[2][:, :heads, :dec_seq], 1, 2)

    return (y_p, y_s, new_mem_k, new_mem_v, fox_k_p, fox_v_p, fox_logf_p,
            fox_k_s, fox_v_s, fox_logf_s, jnp.stack(gmlp_v_s))
```

```python
import functools

import jax
import jax.numpy as jnp
from jax import lax
from jax.experimental import pallas as pl
from jax.experimental.pallas import tpu as pltpu

EPS = 1e-6
NEG_INF = -1e30
LANES = 128
HEAD_DIM = 128
MEM_HEADS = 4
MEM_WIDTH = MEM_HEADS * HEAD_DIM
GMLP_GROUPS = 4
GMLP_CHUNK = 128
PEER_HEADS = 8
PEER_TOPK = 16
PEER_NKEYS = 128
PEER_PAIRS = PEER_HEADS * PEER_TOPK
PEER_UNROLL = 32
FOX_HEADS_PAD = 16
VMEM_LIMIT = 56 * 1024 * 1024
BF16 = jnp.bfloat16
F32 = jnp.float32
NT = (((1,), (1,)), ((), ()))


def _params(*sem):
    return pltpu.CompilerParams(dimension_semantics=sem, vmem_limit_bytes=VMEM_LIMIT)


def _rms(x, g):
    return x * lax.rsqrt(jnp.mean(x * x, axis=-1, keepdims=True) + EPS) * g


NORM_MATMUL_WEIGHT_BYTES = 16 * 1024 * 1024


def _norm_matmul_kernel(x_ref, g_ref, w_ref, o_ref, *rest, emit_norm, emit_bf16):
    xn_ref = rest[-1]

    @pl.when(pl.program_id(1) == 0)
    def _():
        h = _rms(x_ref[...], g_ref[...])
        xn_ref[...] = h.astype(BF16)
        if emit_norm:
            rest[0][...] = h

    y = jnp.dot(xn_ref[...], w_ref[...], preferred_element_type=F32)
    o_ref[...] = y.astype(o_ref.dtype)
    if emit_bf16:
        rest[-2][...] = y.astype(BF16)


def norm_matmul(x, g, w, *, out_dtype=F32, emit_norm=False, emit_bf16=False):
    n, d = x.shape
    dout = w.shape[1]
    tm = min(512, n)
    tn = dout if d * dout * 2 <= NORM_MATMUL_WEIGHT_BYTES else 512
    assert n % tm == 0 and dout % tn == 0
    out_specs = [pl.BlockSpec((tm, tn), lambda i, j: (i, j))]
    out_shape = [jax.ShapeDtypeStruct((n, dout), out_dtype)]
    if emit_norm:
        out_specs.append(pl.BlockSpec((tm, d), lambda i, j: (i, 0)))
        out_shape.append(jax.ShapeDtypeStruct((n, d), F32))
    if emit_bf16:
        out_specs.append(pl.BlockSpec((tm, tn), lambda i, j: (i, j)))
        out_shape.append(jax.ShapeDtypeStruct((n, dout), BF16))
    out = pl.pallas_call(
        functools.partial(_norm_matmul_kernel, emit_norm=emit_norm, emit_bf16=emit_bf16),
        grid=(n // tm, dout // tn),
        in_specs=[
            pl.BlockSpec((tm, d), lambda i, j: (i, 0)),
            pl.BlockSpec((1, d), lambda i, j: (0, 0)),
            pl.BlockSpec((d, tn), lambda i, j: (0, j)),
        ],
        out_specs=out_specs,
        out_shape=out_shape,
        scratch_shapes=[pltpu.VMEM((tm, d), BF16)],
        compiler_params=_params("parallel", "arbitrary"),
        name="norm_matmul",
    )(x, g.reshape(1, d), w)
    return out[0] if len(out) == 1 else tuple(out)


def _gmlp_kernel(zu_ref, zv_ref, w_ref, b_ref, gv_ref, mix_ref, v_ref, *, chunk, n_chunks, gdim):
    tril = (lax.broadcasted_iota(jnp.int32, (chunk, chunk), 1)
            <= lax.broadcasted_iota(jnp.int32, (chunk, chunk), 0))
    for c in range(n_chunks):
        rows = slice(c * chunk, (c + 1) * chunk)
        u = jax.nn.gelu(zu_ref[rows, :])
        v = _rms(jax.nn.gelu(zv_ref[rows, :]), gv_ref[...])
        v_ref[rows, :] = v
        v16 = v.astype(BF16)
        for g in range(GMLP_GROUPS):
            cols = slice(g * gdim, (g + 1) * gdim)
            w = jnp.where(tril, w_ref[g], 0.0).astype(BF16)
            mixed = jnp.dot(w, v16[:, cols], preferred_element_type=F32) + b_ref[g]
            mix_ref[rows, cols] = (u[:, cols] * mixed).astype(BF16)


def gmlp_mix(z, w_s, b_s, g_v, *, chunk, width):
    n = z.shape[0]
    tm = min(512, n)
    assert n % tm == 0 and tm % chunk == 0
    gdim = width // GMLP_GROUPS
    w = w_s[:, :chunk, :chunk]
    b = b_s[:, :chunk].reshape(GMLP_GROUPS, chunk, 1)
    kern = functools.partial(_gmlp_kernel, chunk=chunk, n_chunks=tm // chunk, gdim=gdim)
    return pl.pallas_call(
        kern,
        grid=(n // tm,),
        in_specs=[
            pl.BlockSpec((tm, width), lambda i: (i, 0)),
            pl.BlockSpec((tm, width), lambda i: (i, 1)),
            pl.BlockSpec((GMLP_GROUPS, chunk, chunk), lambda i: (0, 0, 0)),
            pl.BlockSpec((GMLP_GROUPS, chunk, 1), lambda i: (0, 0, 0)),
            pl.BlockSpec((1, width), lambda i: (0, 0)),
        ],
        out_specs=[
            pl.BlockSpec((tm, width), lambda i: (i, 0)),
            pl.BlockSpec((tm, width), lambda i: (i, 0)),
        ],
        out_shape=[jax.ShapeDtypeStruct((n, width), BF16), jax.ShapeDtypeStruct((n, width), F32)],
        compiler_params=_params("parallel"),
        name="gmlp_mix",
    )(z, z, w, b, g_v.reshape(1, width))


def _mix_out_kernel(part_ref, qm_ref, mk_ref, mv_ref, wo_ref, x_ref, o_ref, *, n_sub, t_sub, width):
    scale = HEAD_DIM ** -0.5
    subs = []
    for bi in range(n_sub):
        rows = slice(bi * t_sub, (bi + 1) * t_sub)
        heads = []
        for h in range(MEM_HEADS):
            cols = slice(h * HEAD_DIM, (h + 1) * HEAD_DIM)
            q = qm_ref[rows, cols].astype(BF16)
            s = lax.dot_general(q, mk_ref[bi, :, cols], NT, preferred_element_type=F32) * scale
            s = s - jnp.max(s, axis=-1, keepdims=True)
            p = jnp.exp(s)
            p = (p / jnp.sum(p, axis=-1, keepdims=True)).astype(BF16)
            heads.append(jnp.dot(p, mv_ref[bi, :, cols], preferred_element_type=F32))
        subs.append(jnp.concatenate(heads, axis=1))
    mem = (subs[0] if n_sub == 1 else jnp.concatenate(subs, axis=0)).astype(BF16)
    y = jnp.dot(part_ref[...], wo_ref[:width, :], preferred_element_type=F32)
    y = y + jnp.dot(mem, wo_ref[width:, :], preferred_element_type=F32)
    o_ref[...] = x_ref[...] + y


def mix_out(part, z, q_block, mk, mv, w_o, x, *, seq):
    n, d = x.shape
    width = part.shape[1]
    n_mem = mk.shape[1]
    tm = min(256, n)
    assert n % tm == 0
    if seq >= tm:
        assert seq % tm == 0
        n_sub, t_sub = 1, tm
        kv_map = lambda i: ((i * tm) // seq, 0, 0)
    else:
        assert tm % seq == 0
        n_sub, t_sub = tm // seq, seq
        kv_map = lambda i: (i, 0, 0)
    kern = functools.partial(_mix_out_kernel, n_sub=n_sub, t_sub=t_sub, width=width)
    return pl.pallas_call(
        kern,
        grid=(n // tm,),
        in_specs=[
            pl.BlockSpec((tm, width), lambda i: (i, 0)),
            pl.BlockSpec((tm, MEM_WIDTH), lambda i: (i, q_block)),
            pl.BlockSpec((n_sub, n_mem, MEM_WIDTH), kv_map),
            pl.BlockSpec((n_sub, n_mem, MEM_WIDTH), kv_map),
            pl.BlockSpec((width + MEM_WIDTH, d), lambda i: (0, 0)),
            pl.BlockSpec((tm, d), lambda i: (i, 0)),
        ],
        out_specs=pl.BlockSpec((tm, d), lambda i: (i, 0)),
        out_shape=jax.ShapeDtypeStruct((n, d), F32),
        compiler_params=_params("parallel"),
        name="mix_out",
    )(part, z, mk, mv, w_o, x)


def _log_sigmoid(x):
    return jnp.minimum(x, 0.0) - jnp.log1p(jnp.exp(-jnp.abs(x)))


def _cumsum_kernel(*refs, has_prev, t_new):
    if has_prev:
        prev_ref, raw_ref, b_ref, c_ref, lf_ref = refs
    else:
        raw_ref, b_ref, c_ref, lf_ref = refs
    raw = raw_ref[0]
    lf = _log_sigmoid(raw + b_ref[...])
    lf = jnp.where(lax.broadcasted_iota(jnp.int32, lf.shape, 1) < t_new, lf, 0.0)
    lf_ref[0] = lf
    full = jnp.concatenate([prev_ref[0], lf], axis=1) if has_prev else lf
    total = full.shape[1]
    lane = lax.broadcasted_iota(jnp.int32, full.shape, 1)
    shift = 1
    while shift < total:
        full = full + jnp.where(lane >= shift, pltpu.roll(full, shift, 1), 0.0)
        shift *= 2
    c_ref[0] = full


def logf_cumsum(raw_t, b_col, t_new, prev_t=None):
    nb, hp, t_pad = raw_t.shape
    past = 0 if prev_t is None else prev_t.shape[2]
    kern = functools.partial(_cumsum_kernel, has_prev=prev_t is not None, t_new=t_new)
    in_specs = [pl.BlockSpec((1, hp, t_pad), lambda i: (i, 0, 0)), pl.BlockSpec((hp, 1), lambda i: (0, 0))]
    args = [raw_t, b_col]
    if prev_t is not None:
        in_specs = [pl.BlockSpec((1, hp, past), lambda i: (i, 0, 0))] + in_specs
        args = [prev_t] + args
    return pl.pallas_call(
        kern,
        grid=(nb,),
        in_specs=in_specs,
        out_specs=[
            pl.BlockSpec((1, hp, past + t_pad), lambda i: (i, 0, 0)),
            pl.BlockSpec((1, hp, t_pad), lambda i: (i, 0, 0)),
        ],
        out_shape=[jax.ShapeDtypeStruct((nb, hp, past + t_pad), F32), jax.ShapeDtypeStruct((nb, hp, t_pad), F32)],
        compiler_params=_params("parallel"),
        name="logf_cumsum",
    )(*args)


FOX_TILE = 2048
FOX_HEADS_PER_STEP = 2


def _fox_kernel(qi_ref, ki_ref, q_ref, k_ref, v_ref, ck_ref, cq_ref, o_ref, m_ref, l_ref, acc_ref, *, tile):
    step = pl.program_id(2)
    qi = qi_ref[step]
    ki = ki_ref[step]

    @pl.when(ki == 0)
    def _():
        m_ref[...] = jnp.full(m_ref.shape, NEG_INF, F32)
        l_ref[...] = jnp.zeros(l_ref.shape, F32)
        acc_ref[...] = jnp.zeros(acc_ref.shape, F32)

    def block(masked):
        for hh in range(FOX_HEADS_PER_STEP):
            cols = slice(hh * HEAD_DIM, (hh + 1) * HEAD_DIM)
            s = lax.dot_general(q_ref[:, cols], k_ref[:, cols], NT, preferred_element_type=F32) * (HEAD_DIM ** -0.5)
            s = s - (ck_ref[0, hh] - cq_ref[0, hh, :, 0:1])
            if masked:
                qpos = lax.broadcasted_iota(jnp.int32, (tile, tile), 0)
                kpos = lax.broadcasted_iota(jnp.int32, (tile, tile), 1)
                s = jnp.where(kpos <= qpos, s, NEG_INF)
            m_prev = m_ref[hh]
            m_new = jnp.maximum(m_prev, jnp.max(s, axis=-1, keepdims=True))
            alpha = jnp.exp(m_prev - m_new)
            p = jnp.exp(s - m_new)
            l_ref[hh] = alpha * l_ref[hh] + jnp.sum(p, axis=-1, keepdims=True)
            acc_ref[:, cols] = alpha * acc_ref[:, cols] + jnp.dot(p.astype(BF16), v_ref[:, cols],
                                                                  preferred_element_type=F32)
            m_ref[hh] = m_new

    @pl.when(ki < qi)
    def _():
        block(False)

    @pl.when(ki == qi)
    def _():
        block(True)
        for hh in range(FOX_HEADS_PER_STEP):
            cols = slice(hh * HEAD_DIM, (hh + 1) * HEAD_DIM)
            o_ref[:, cols] = (acc_ref[:, cols] / l_ref[hh]).astype(o_ref.dtype)


def fox_prompt(zq, k, v, c, *, batch, seq, heads):
    n = zq.shape[0]
    tile = min(FOX_TILE, seq)
    nt = seq // tile
    hs = FOX_HEADS_PER_STEP
    assert heads % hs == 0
    width = hs * HEAD_DIM
    pairs = [(qi, ki) for qi in range(nt) for ki in range(qi + 1)]
    qi_tab = jnp.asarray([p[0] for p in pairs], jnp.int32)
    ki_tab = jnp.asarray([p[1] for p in pairs], jnp.int32)
    q_map = lambda b, h, s, qt, kt: (b * nt + qt[s], h)
    kv_map = lambda b, h, s, qt, kt: (b * nt + kt[s], h)
    grid_spec = pltpu.PrefetchScalarGridSpec(
        num_scalar_prefetch=2,
        grid=(batch, heads // hs, len(pairs)),
        in_specs=[
            pl.BlockSpec((tile, width), q_map),
            pl.BlockSpec((tile, width), kv_map),
            pl.BlockSpec((tile, width), kv_map),
            pl.BlockSpec((1, hs, 1, tile), lambda b, h, s, qt, kt: (b, h, 0, kt[s])),
            pl.BlockSpec((1, hs, 1, tile), lambda b, h, s, qt, kt: (b, h, 0, qt[s])),
        ],
        out_specs=pl.BlockSpec((tile, width), q_map),
        scratch_shapes=[pltpu.VMEM((hs, tile, 1), F32), pltpu.VMEM((hs, tile, 1), F32),
                        pltpu.VMEM((tile, width), F32)],
    )
    return pl.pallas_call(
        functools.partial(_fox_kernel, tile=tile),
        grid_spec=grid_spec,
        out_shape=jax.ShapeDtypeStruct((n, heads * HEAD_DIM), BF16),
        compiler_params=_params("parallel", "parallel", "arbitrary"),
        name="fox_prompt",
    )(qi_tab, ki_tab, zq, k, v, c, c)


def _fox_sample_kernel(q_ref, kc_ref, vc_ref, kn_ref, vn_ref, c_ref, o_ref, *, past, t_new):
    scale = HEAD_DIM ** -0.5
    q = q_ref[...].astype(BF16)
    c = c_ref[0, 0]
    c_ref0 = c[:, past:past + 1]
    s_old = lax.dot_general(q, kc_ref[0].astype(BF16), NT, preferred_element_type=F32) * scale
    s_old = s_old - (c[:, :past] - c_ref0)
    s_new = lax.dot_general(q, kn_ref[...].astype(BF16), NT, preferred_element_type=F32) * scale
    s_new = s_new - (c[:, past:past + t_new] - c_ref0)
    causal = (lax.broadcasted_iota(jnp.int32, (t_new, t_new), 1)
              <= lax.broadcasted_iota(jnp.int32, (t_new, t_new), 0))
    s_new = jnp.where(causal, s_new, NEG_INF)
    m = jnp.maximum(jnp.max(s_old, axis=-1, keepdims=True), jnp.max(s_new, axis=-1, keepdims=True))
    p_old = jnp.exp(s_old - m)
    p_new = jnp.exp(s_new - m)
    denom = jnp.sum(p_old, axis=-1, keepdims=True) + jnp.sum(p_new, axis=-1, keepdims=True)
    acc = jnp.dot(p_old.astype(BF16), vc_ref[0].astype(BF16), preferred_element_type=F32)
    acc = acc + jnp.dot(p_new.astype(BF16), vn_ref[...].astype(BF16), preferred_element_type=F32)
    o_ref[...] = (acc / denom).astype(o_ref.dtype)


def fox_sample(zq, k_cache, v_cache, k_new, v_new, c, *, batch, t_new, heads):
    past = k_cache.shape[1]
    c_len = c.shape[3]
    kern = functools.partial(_fox_sample_kernel, past=past, t_new=t_new)
    return pl.pallas_call(
        kern,
        grid=(batch, heads),
        in_specs=[
            pl.BlockSpec((t_new, HEAD_DIM), lambda b, h: (b, h)),
            pl.BlockSpec((1, past, HEAD_DIM), lambda b, h: (b, 0, h)),
            pl.BlockSpec((1, past, HEAD_DIM), lambda b, h: (b, 0, h)),
            pl.BlockSpec((t_new, HEAD_DIM), lambda b, h: (b, h)),
            pl.BlockSpec((t_new, HEAD_DIM), lambda b, h: (b, h)),
            pl.BlockSpec((1, 1, 1, c_len), lambda b, h: (b, h, 0, 0)),
        ],
        out_specs=pl.BlockSpec((t_new, HEAD_DIM), lambda b, h: (b, h)),
        out_shape=jax.ShapeDtypeStruct((batch * t_new, heads * HEAD_DIM), BF16),
        compiler_params=_params("parallel", "parallel"),
        name="fox_sample",
    )(zq, k_cache, v_cache, k_new, v_new, c)


def _top_rows(vals_list, count, payloads=None):
    n = len(vals_list)
    rows = [lax.broadcasted_iota(jnp.int32, v.shape, 0) for v in vals_list]
    vals = list(vals_list)
    out_v = [[] for _ in range(n)]
    out_i = [[] for _ in range(n)]
    for _ in range(count):
        for p in range(n):
            m = jnp.max(vals[p], axis=0, keepdims=True)
            idx = jnp.min(jnp.where(vals[p] == m, rows[p], vals[p].shape[0]), axis=0, keepdims=True)
            hit = rows[p] == idx
            out_v[p].append(m)
            out_i[p].append(idx if payloads is None
                            else jnp.max(jnp.where(hit, payloads[p], -1), axis=0, keepdims=True))
            vals[p] = jnp.where(hit, -jnp.inf, vals[p])
    return out_v, out_i


ROUTE_HEADS_PER_STEP = 4


def _peer_route_kernel(q_ref, keys_ref, eidx_ref, gate_ref, e_scr, g_scr):
    step = pl.program_id(1)
    q = q_ref[...].astype(BF16)
    hs = ROUTE_HEADS_PER_STEP
    scores = [lax.dot_general(keys_ref[hh, p].astype(BF16), q[:, (2 * hh + p) * LANES:(2 * hh + p + 1) * LANES], NT,
                              preferred_element_type=F32) for hh in range(hs) for p in range(2)]
    top_v, top_i = _top_rows(scores, PEER_TOPK)
    half = PEER_TOPK // 2
    sub = lax.broadcasted_iota(jnp.int32, (half, q.shape[0]), 0)
    cands, cidxs = [], []
    for hh in range(hs):
        hv0, hi0, hv1, hi1 = top_v[2 * hh], top_i[2 * hh], top_v[2 * hh + 1], top_i[2 * hh + 1]
        v0 = jnp.concatenate(hv0, axis=0)
        i0 = jnp.concatenate(hi0, axis=0)
        v1 = jnp.concatenate(hv1, axis=0)
        i1 = jnp.concatenate(hi1, axis=0)
        cand = [hv0[0] + v1]
        cidx = [hi0[0] * PEER_NKEYS + i1]
        for a in range(1, half):
            keep = sub < PEER_TOPK // (a + 1)
            cand.append(jnp.where(keep, hv0[a] + v1[:half], -jnp.inf))
            cidx.append(hi0[a] * PEER_NKEYS + i1[:half])
        cand.append(v0[half:] + hv1[0])
        cidx.append(i0[half:] * PEER_NKEYS + hi1[0])
        cands.append(jnp.concatenate(cand, axis=0))
        cidxs.append(jnp.concatenate(cidx, axis=0))
    bests, eidxs = _top_rows(cands, PEER_TOPK, payloads=cidxs)
    for hh in range(hs):
        best = jnp.concatenate(bests[hh], axis=0)
        e = jnp.exp(best - jnp.max(best, axis=0, keepdims=True))
        gate = e / jnp.sum(e, axis=0, keepdims=True)
        row0 = pl.multiple_of((step * hs + hh) * PEER_TOPK, PEER_TOPK)
        e_scr[pl.ds(row0, PEER_TOPK), :] = jnp.concatenate(eidxs[hh], axis=0)
        g_scr[pl.ds(row0, PEER_TOPK), :] = gate

    @pl.when(step == pl.num_programs(1) - 1)
    def _():
        eidx_ref[...] = e_scr[...].T
        gate_ref[...] = g_scr[...].T


def peer_route(q, sub_keys):
    n = q.shape[0]
    tt = min(256, n)
    hs = ROUTE_HEADS_PER_STEP
    assert n % tt == 0 and PEER_HEADS % hs == 0
    return pl.pallas_call(
        _peer_route_kernel,
        grid=(n // tt, PEER_HEADS // hs),
        in_specs=[
            pl.BlockSpec((tt, 2 * hs * LANES), lambda i, h: (i, h)),
            pl.BlockSpec((hs, 2, PEER_NKEYS, LANES), lambda i, h: (h, 0, 0, 0)),
        ],
        out_specs=[
            pl.BlockSpec((tt, PEER_PAIRS), lambda i, h: (i, 0)),
            pl.BlockSpec((tt, PEER_PAIRS), lambda i, h: (i, 0)),
        ],
        out_shape=[jax.ShapeDtypeStruct((n, PEER_PAIRS), jnp.int32), jax.ShapeDtypeStruct((n, PEER_PAIRS), F32)],
        scratch_shapes=[pltpu.VMEM((PEER_PAIRS, tt), jnp.int32), pltpu.VMEM((PEER_PAIRS, tt), F32)],
        compiler_params=_params("parallel", "arbitrary"),
        name="peer_route",
    )(q, sub_keys)


PEER_SPLIT = 8
PEER_CHUNK = PEER_SPLIT * LANES
PEER_ROWS = PEER_PAIRS * PEER_SPLIT
PACKED_ROWS = PEER_SPLIT // 2


def _peer_pass_kernel(*refs, mode, group, chunk):
    if mode == "u_first":
        idx_ref, h_ref, slab_hbm, o_ref, slab, stage, tiles, sem = refs
    elif mode == "u_last":
        idx_ref, h_ref, w_ref, gate_ref, slab_hbm, o_ref, slab, stage, tiles, sem = refs
    else:
        idx_ref, coef_ref, x_ref, slab_hbm, o_ref, slab, stage, tiles, sem = refs

    @pl.when(pl.program_id(0) == 0)
    def _():
        load = pltpu.make_async_copy(slab_hbm, slab, sem.at[0])
        load.start()
        load.wait()

    if mode != "v":
        for r in range(PEER_SPLIT):
            tiles[pl.ds(r, group, stride=PEER_SPLIT), :] = h_ref[:, r * LANES:(r + 1) * LANES]

    lane = lax.broadcasted_iota(jnp.int32, (PEER_SPLIT, PEER_ROWS), 1)
    diag = (lane % PEER_SPLIT) == lax.broadcasted_iota(jnp.int32, (PEER_SPLIT, PEER_ROWS), 0)

    def one_token(t, j):
        for k in range(PEER_PAIRS):
            row = pl.multiple_of(idx_ref[t, k], PACKED_ROWS)
            stage[j, pl.ds(k * PACKED_ROWS, PACKED_ROWS), :] = slab[pl.ds(row, PACKED_ROWS), :]
        m16 = pltpu.bitcast(stage[j], BF16)
        r0 = pl.multiple_of(t * PEER_SPLIT, PEER_SPLIT)
        if mode == "v":
            coef = coef_ref[pl.ds(t, 1), :]
            lhs = jnp.where(diag, jnp.broadcast_to(coef, (PEER_SPLIT, PEER_ROWS)), 0.0).astype(BF16)
            tiles[pl.ds(r0, PEER_SPLIT), :] = jnp.dot(lhs, m16, preferred_element_type=F32)
            return
        h16 = tiles[pl.ds(r0, PEER_SPLIT), :].astype(BF16)
        yt = lax.dot_general(h16, m16, NT, preferred_element_type=F32)
        o_ref[pl.ds(t, 1), :] = jnp.sum(jnp.where(diag, yt, 0.0), axis=0, keepdims=True)

    def body(i, carry):
        for j in range(PEER_UNROLL):
            one_token(i * PEER_UNROLL + j, j)
        return carry

    lax.fori_loop(0, group // PEER_UNROLL, body, 0)
    if mode == "v":
        lo, hi = chunk * PEER_CHUNK, (chunk + 1) * PEER_CHUNK
        if lo > 0:
            o_ref[:, :lo] = x_ref[:, :lo]
        if hi < x_ref.shape[1]:
            o_ref[:, hi:] = x_ref[:, hi:]
        for r in range(PEER_SPLIT):
            cols = slice(lo + r * LANES, lo + (r + 1) * LANES)
            o_ref[:, cols] = x_ref[:, cols] + tiles[pl.ds(r, group, stride=PEER_SPLIT), :]
    if mode == "u_last":
        w = o_ref[...] + w_ref[...]
        lane_g = lax.broadcasted_iota(jnp.int32, w.shape, 1)
        for s in (1, 2, 4):
            up = pltpu.roll(w, s, 1)
            down = pltpu.roll(w, PEER_ROWS - s, 1)
            w = w + jnp.where((lane_g & s) != 0, up, down)
        o_ref[...] = gate_ref[...] * jax.nn.gelu(w)


def peer_pass(mode, idx, slab, *operands, group=64):
    n = idx.shape[0]
    group = min(group, n)
    assert n % group == 0 and group % PEER_UNROLL == 0
    vec = pl.BlockSpec((group, PEER_ROWS), lambda i: (i, 0))
    idx_spec = pl.BlockSpec((group, PEER_PAIRS), lambda i: (i, 0), memory_space=pltpu.SMEM)
    any_spec = pl.BlockSpec(memory_space=pl.ANY)
    chunk = operands[-1]
    cols = pl.BlockSpec((group, PEER_CHUNK), lambda i: (i, chunk))
    if mode == "v":
        coef, x = operands[:2]
        full = pl.BlockSpec((group, x.shape[1]), lambda i: (i, 0))
        in_specs, args = [idx_spec, vec, full, any_spec], (idx, coef, x, slab)
        out_spec, out_shape = full, jax.ShapeDtypeStruct(x.shape, F32)
    else:
        h = operands[0]
        extra = list(operands[1:-1])
        in_specs, args = [idx_spec, cols] + [vec] * len(extra) + [any_spec], (idx, h, *extra, slab)
        out_spec, out_shape = vec, jax.ShapeDtypeStruct((n, PEER_ROWS), F32)
    return pl.pallas_call(
        functools.partial(_peer_pass_kernel, mode=mode, group=group, chunk=chunk),
        grid=(n // group,),
        in_specs=in_specs,
        out_specs=out_spec,
        out_shape=out_shape,
        scratch_shapes=[
            pltpu.VMEM(slab.shape, jnp.uint32),
            pltpu.VMEM((PEER_UNROLL, PEER_PAIRS * PACKED_ROWS, LANES), jnp.uint32),
            pltpu.VMEM((group * PEER_SPLIT, LANES), F32),
            pltpu.SemaphoreType.DMA((1,)),
        ],
        compiler_params=_params("arbitrary"),
        name="peer_" + mode,
    )(*args)


def peer_table(u, v):
    def slabs(t):
        e, d = t.shape
        bits = lax.bitcast_convert_type(t.astype(BF16), jnp.uint16).astype(jnp.uint32)
        bits = bits.reshape(e, d // PEER_CHUNK, PACKED_ROWS, 2, LANES)
        packed = bits[:, :, :, 0, :] | (bits[:, :, :, 1, :] << 16)
        return [packed[:, c].reshape(e * PACKED_ROWS, LANES) for c in range(d // PEER_CHUNK)]
    return slabs(u), slabs(v)


def peer_apply(x, h, eidx, gate, tables):
    u_slabs, v_slabs = tables
    idx = eidx * PACKED_ROWS
    gate8 = jnp.repeat(gate, PEER_SPLIT, axis=1)
    w = None
    for c, slab in enumerate(u_slabs):
        if c < len(u_slabs) - 1:
            assert w is None
            w = peer_pass("u_first", idx, slab, h, c)
        else:
            w = peer_pass("u_last", idx, slab, h, w, gate8, c)
    y = x
    for c, slab in enumerate(v_slabs):
        y = peer_pass("v", idx, slab, w, y, c)
    return y


def _rms_kernel(x_ref, g_ref, o_ref):
    o_ref[...] = _rms(x_ref[...], g_ref[...])


def rms_norm(x, g):
    n, d = x.shape
    tm = min(512, n)
    return pl.pallas_call(
        _rms_kernel,
        grid=(n // tm,),
        in_specs=[pl.BlockSpec((tm, d), lambda i: (i, 0)), pl.BlockSpec((1, d), lambda i: (0, 0))],
        out_specs=pl.BlockSpec((tm, d), lambda i: (i, 0)),
        out_shape=jax.ShapeDtypeStruct((n, d), F32),
        compiler_params=_params("parallel"),
        name="rms_norm",
    )(x, g.reshape(1, d))


def peer_ffn(x, g, w_pq16, sub_keys, tables, g_final=None):
    q, h = norm_matmul(x, g, w_pq16, emit_norm=True)
    eidx, gate = peer_route(q, sub_keys)
    y = peer_apply(x, h, eidx, gate, tables)
    return y if g_final is None else rms_norm(y, g_final)


def _forget_inputs(zf, batch, t_new, t_pad):
    raw = zf[:, :FOX_HEADS_PAD].reshape(batch, t_new, FOX_HEADS_PAD)
    raw = jnp.swapaxes(raw, 1, 2)
    return jnp.pad(raw, ((0, 0), (0, 0), (0, t_pad - t_new)))


def kernel(x_prompt, x_sample, mem_prompt, cache_mem_k, cache_mem_v, cache_fox_k, cache_fox_v, cache_fox_logf, g_mix, g_ffn, w_o, g_mem, w_mem_kv, w_pq, peer_sub_keys, peer_u, peer_v, w_in_a, gmlp_ws, gmlp_b, gmlp_gv, w_in_b, g_kv, w_kv, b_f, g_final):
    batch, seq, d = x_prompt.shape
    dec_batch, dec_seq, _ = x_sample.shape
    depth = g_mix.shape[0]
    n_a = w_in_a.shape[0]
    n_mem = mem_prompt.shape[1]
    past = cache_fox_k.shape[1]
    heads = cache_fox_k.shape[2]
    fox_width = heads * HEAD_DIM
    gmlp_width = gmlp_gv.shape[1]
    assert gmlp_width == fox_width and d == fox_width + MEM_WIDTH

    w_o16 = w_o.astype(BF16)
    w_pq16 = w_pq.astype(BF16)
    w_in_a16 = w_in_a.astype(BF16)
    w_in_b16 = w_in_b.astype(BF16)
    w_mem16 = w_mem_kv.astype(BF16)
    w_k16 = w_kv[:, :fox_width].astype(BF16)
    w_v16 = w_kv[:, fox_width:2 * fox_width].astype(BF16)
    w_f16 = jnp.pad(w_kv[:, 2 * fox_width:], ((0, 0), (0, LANES - heads))).astype(BF16)
    b_col = jnp.pad(b_f, (0, FOX_HEADS_PAD - heads)).reshape(FOX_HEADS_PAD, 1)
    uv = [peer_table(peer_u[l], peer_v[l]) for l in range(depth)]

    def ffn(x, l, final=False):
        return peer_ffn(x, g_ffn[l], w_pq16[l], peer_sub_keys[l], uv[l], g_final if final else None)

    def shared_kv(x):
        k, k16 = norm_matmul(x, g_kv, w_k16, emit_bf16=True)
        v, v16 = norm_matmul(x, g_kv, w_v16, emit_bf16=True)
        zf = norm_matmul(x, g_kv, w_f16)
        return k, v, zf, k16, v16

    def run_group(x3, mem_k, mem_v, chunk, fox_fn):
        b, t, _ = x3.shape
        x = x3.reshape(b * t, d)
        gmlp_v = []
        extras = None
        for l in range(depth):
            if l < n_a:
                z = norm_matmul(x, g_mix[l], w_in_a16[l])
                part, v_rows = gmlp_mix(z, gmlp_ws[l], gmlp_b[l], gmlp_gv[l], chunk=chunk, width=gmlp_width)
                gmlp_v.append(v_rows.reshape(b, t, gmlp_width))
                x = mix_out(part, z, 2 * gmlp_width // MEM_WIDTH, mem_k[l], mem_v[l], w_o16[l], x, seq=t)
                x = ffn(x, l)
                if l == n_a - 1:
                    kvf = shared_kv(x)
            else:
                z = norm_matmul(x, g_mix[l], w_in_b16[l - n_a], out_dtype=BF16)
                part, extras = fox_fn(z, kvf, extras)
                x = mix_out(part, z, fox_width // MEM_WIDTH, mem_k[l], mem_v[l], w_o16[l], x, seq=t)
                x = ffn(x, l, final=(l == depth - 1))
        return x.reshape(b, t, d), gmlp_v, extras

    mem2d = mem_prompt.reshape(batch * n_mem, d)
    mem_k_p, mem_v_p = [], []
    for l in range(depth):
        zkv = norm_matmul(mem2d, g_mem[l], w_mem16[l])
        mem_k_p.append(zkv[:, :MEM_WIDTH].reshape(batch, n_mem, MEM_WIDTH))
        mem_v_p.append(zkv[:, MEM_WIDTH:].reshape(batch, n_mem, MEM_WIDTH))

    def fox_p(z, kvf, extras):
        if extras is None:
            k, v, zf, k16, v16 = kvf
            c, lf = logf_cumsum(_forget_inputs(zf, batch, seq, seq), b_col, seq)
            extras = (k, v, lf, c[:, :heads].reshape(batch, heads, 1, seq), k16, v16)
        k, v, lf, c, k16, v16 = extras
        return fox_prompt(z, k16, v16, c, batch=batch, seq=seq, heads=heads), extras

    y_p, _, ex_p = run_group(x_prompt, [a.astype(BF16) for a in mem_k_p], [a.astype(BF16) for a in mem_v_p],
                             min(seq, GMLP_CHUNK), fox_p)
    fox_k_p = ex_p[0].reshape(batch, seq, heads, HEAD_DIM)
    fox_v_p = ex_p[1].reshape(batch, seq, heads, HEAD_DIM)
    fox_logf_p = jnp.swapaxes(ex_p[2][:, :heads, :seq], 1, 2)
    new_mem_k = jnp.stack(mem_k_p).reshape(depth, batch, n_mem, MEM_HEADS, HEAD_DIM)
    new_mem_v = jnp.stack(mem_v_p).reshape(depth, batch, n_mem, MEM_HEADS, HEAD_DIM)

    t_pad = -(-dec_seq // LANES) * LANES
    prev_t = jnp.pad(jnp.swapaxes(cache_fox_logf, 1, 2), ((0, 0), (0, FOX_HEADS_PAD - heads), (0, 0)))
    k_cache = cache_fox_k.reshape(dec_batch, past, fox_width)
    v_cache = cache_fox_v.reshape(dec_batch, past, fox_width)

    def fox_s(z, kvf, extras):
        if extras is None:
            k, v, zf = kvf[:3]
            c, lf = logf_cumsum(_forget_inputs(zf, dec_batch, dec_seq, t_pad), b_col, dec_seq, prev_t)
            extras = (k, v, lf, c[:, :heads].reshape(dec_batch, heads, 1, past + t_pad))
        k, v, lf, c = extras
        return fox_sample(z, k_cache, v_cache, k, v, c, batch=dec_batch, t_new=dec_seq, heads=heads), extras

    mem_k_s = [cache_mem_k[l].reshape(dec_batch, n_mem, MEM_WIDTH).astype(BF16) for l in range(depth)]
    mem_v_s = [cache_mem_v[l].reshape(dec_batch, n_mem, MEM_WIDTH).astype(BF16) for l in range(depth)]
    y_s, gmlp_v_s, ex_s = run_group(x_sample, mem_k_s, mem_v_s, min(dec_seq, GMLP_CHUNK), fox_s)
    fox_k_s = ex_s[0].reshape(dec_batch, dec_seq, heads, HEAD_DIM)
    fox_v_s = ex_s[1].reshape(dec_batch, dec_seq, heads, HEAD_DIM)
    fox_logf_s = jnp.swapaxes(ex_s[2][:, :heads, :dec_seq], 1, 2)

    return (y_p, y_s, new_mem_k, new_mem_v, fox_k_p, fox_v_p, fox_logf_p,
            fox_k_s, fox_v_s, fox_logf_s, jnp.stack(gmlp_v_s))
```

```python
import functools

import jax
import jax.numpy as jnp
from jax import lax
from jax.experimental import pallas as pl
from jax.experimental.pallas import tpu as pltpu

EPS = 1e-6
NEG_INF = -1e30
LANES = 128
HEAD_DIM = 128
MEM_HEADS = 4
MEM_WIDTH = MEM_HEADS * HEAD_DIM
GMLP_GROUPS = 4
GMLP_CHUNK = 128
PEER_HEADS = 8
PEER_TOPK = 16
PEER_NKEYS = 128
PEER_PAIRS = PEER_HEADS * PEER_TOPK
PEER_UNROLL = 32
FOX_HEADS_PAD = 16
VMEM_LIMIT = 56 * 1024 * 1024
BF16 = jnp.bfloat16
F32 = jnp.float32
NT = (((1,), (1,)), ((), ()))


def _params(*sem):
    return pltpu.CompilerParams(dimension_semantics=sem, vmem_limit_bytes=VMEM_LIMIT)


def _rms(x, g):
    return x * lax.rsqrt(jnp.mean(x * x, axis=-1, keepdims=True) + EPS) * g


NORM_MATMUL_WEIGHT_BYTES = 16 * 1024 * 1024


def _norm_matmul_kernel(x_ref, g_ref, w_ref, o_ref, *rest, emit_norm, emit_bf16):
    xn_ref = rest[-1]

    @pl.when(pl.program_id(1) == 0)
    def _():
        h = _rms(x_ref[...], g_ref[...])
        xn_ref[...] = h.astype(BF16)
        if emit_norm:
            rest[0][...] = h

    y = jnp.dot(xn_ref[...], w_ref[...], preferred_element_type=F32)
    o_ref[...] = y.astype(o_ref.dtype)
    if emit_bf16:
        rest[-2][...] = y.astype(BF16)


def norm_matmul(x, g, w, *, out_dtype=F32, emit_norm=False, emit_bf16=False):
    n, d = x.shape
    dout = w.shape[1]
    tm = min(512, n)
    tn = dout if d * dout * 2 <= NORM_MATMUL_WEIGHT_BYTES else 512
    assert n % tm == 0 and dout % tn == 0
    out_specs = [pl.BlockSpec((tm, tn), lambda i, j: (i, j))]
    out_shape = [jax.ShapeDtypeStruct((n, dout), out_dtype)]
    if emit_norm:
        out_specs.append(pl.BlockSpec((tm, d), lambda i, j: (i, 0)))
        out_shape.append(jax.ShapeDtypeStruct((n, d), F32))
    if emit_bf16:
        out_specs.append(pl.BlockSpec((tm, tn), lambda i, j: (i, j)))
        out_shape.append(jax.ShapeDtypeStruct((n, dout), BF16))
    out = pl.pallas_call(
        functools.partial(_norm_matmul_kernel, emit_norm=emit_norm, emit_bf16=emit_bf16),
        grid=(n // tm, dout // tn),
        in_specs=[
            pl.BlockSpec((tm, d), lambda i, j: (i, 0)),
            pl.BlockSpec((1, d), lambda i, j: (0, 0)),
            pl.BlockSpec((d, tn), lambda i, j: (0, j)),
        ],
        out_specs=out_specs,
        out_shape=out_shape,
        scratch_shapes=[pltpu.VMEM((tm, d), BF16)],
        compiler_params=_params("parallel", "arbitrary"),
        name="norm_matmul",
    )(x, g.reshape(1, d), w)
    return out[0] if len(out) == 1 else tuple(out)


def _gmlp_kernel(zu_ref, zv_ref, w_ref, b_ref, gv_ref, mix_ref, v_ref, *, chunk, n_chunks, gdim):
    tril = (lax.broadcasted_iota(jnp.int32, (chunk, chunk), 1)
            <= lax.broadcasted_iota(jnp.int32, (chunk, chunk), 0))
    for c in range(n_chunks):
        rows = slice(c * chunk, (c + 1) * chunk)
        u = jax.nn.gelu(zu_ref[rows, :])
        v = _rms(jax.nn.gelu(zv_ref[rows, :]), gv_ref[...])
        v_ref[rows, :] = v
        v16 = v.astype(BF16)
        for g in range(GMLP_GROUPS):
            cols = slice(g * gdim, (g + 1) * gdim)
            w = jnp.where(tril, w_ref[g], 0.0).astype(BF16)
            mixed = jnp.dot(w, v16[:, cols], preferred_element_type=F32) + b_ref[g]
            mix_ref[rows, cols] = (u[:, cols] * mixed).astype(BF16)


def gmlp_mix(z, w_s, b_s, g_v, *, chunk, width):
    n = z.shape[0]
    tm = min(512, n)
    assert n % tm == 0 and tm % chunk == 0
    gdim = width // GMLP_GROUPS
    w = w_s[:, :chunk, :chunk]
    b = b_s[:, :chunk].reshape(GMLP_GROUPS, chunk, 1)
    kern = functools.partial(_gmlp_kernel, chunk=chunk, n_chunks=tm // chunk, gdim=gdim)
    return pl.pallas_call(
        kern,
        grid=(n // tm,),
        in_specs=[
            pl.BlockSpec((tm, width), lambda i: (i, 0)),
            pl.BlockSpec((tm, width), lambda i: (i, 1)),
            pl.BlockSpec((GMLP_GROUPS, chunk, chunk), lambda i: (0, 0, 0)),
            pl.BlockSpec((GMLP_GROUPS, chunk, 1), lambda i: (0, 0, 0)),
            pl.BlockSpec((1, width), lambda i: (0, 0)),
        ],
        out_specs=[
            pl.BlockSpec((tm, width), lambda i: (i, 0)),
            pl.BlockSpec((tm, width), lambda i: (i, 0)),
        ],
        out_shape=[jax.ShapeDtypeStruct((n, width), BF16), jax.ShapeDtypeStruct((n, width), F32)],
        compiler_params=_params("parallel"),
        name="gmlp_mix",
    )(z, z, w, b, g_v.reshape(1, width))


def _mix_out_kernel(part_ref, qm_ref, mk_ref, mv_ref, wo_ref, x_ref, o_ref, *, n_sub, t_sub, width):
    scale = HEAD_DIM ** -0.5
    subs = []
    for bi in range(n_sub):
        rows = slice(bi * t_sub, (bi + 1) * t_sub)
        heads = []
        for h in range(MEM_HEADS):
            cols = slice(h * HEAD_DIM, (h + 1) * HEAD_DIM)
            q = qm_ref[rows, cols].astype(BF16)
            s = lax.dot_general(q, mk_ref[bi, :, cols], NT, preferred_element_type=F32) * scale
            s = s - jnp.max(s, axis=-1, keepdims=True)
            p = jnp.exp(s)
            p = (p / jnp.sum(p, axis=-1, keepdims=True)).astype(BF16)
            heads.append(jnp.dot(p, mv_ref[bi, :, cols], preferred_element_type=F32))
        subs.append(jnp.concatenate(heads, axis=1))
    mem = (subs[0] if n_sub == 1 else jnp.concatenate(subs, axis=0)).astype(BF16)
    y = jnp.dot(part_ref[...], wo_ref[:width, :], preferred_element_type=F32)
    y = y + jnp.dot(mem, wo_ref[width:, :], preferred_element_type=F32)
    o_ref[...] = x_ref[...] + y


def mix_out(part, z, q_block, mk, mv, w_o, x, *, seq):
    n, d = x.shape
    width = part.shape[1]
    n_mem = mk.shape[1]
    tm = min(256, n)
    assert n % tm == 0
    if seq >= tm:
        assert seq % tm == 0
        n_sub, t_sub = 1, tm
        kv_map = lambda i: ((i * tm) // seq, 0, 0)
    else:
        assert tm % seq == 0
        n_sub, t_sub = tm // seq, seq
        kv_map = lambda i: (i, 0, 0)
    kern = functools.partial(_mix_out_kernel, n_sub=n_sub, t_sub=t_sub, width=width)
    return pl.pallas_call(
        kern,
        grid=(n // tm,),
        in_specs=[
            pl.BlockSpec((tm, width), lambda i: (i, 0)),
            pl.BlockSpec((tm, MEM_WIDTH), lambda i: (i, q_block)),
            pl.BlockSpec((n_sub, n_mem, MEM_WIDTH), kv_map),
            pl.BlockSpec((n_sub, n_mem, MEM_WIDTH), kv_map),
            pl.BlockSpec((width + MEM_WIDTH, d), lambda i: (0, 0)),
            pl.BlockSpec((tm, d), lambda i: (i, 0)),
        ],
        out_specs=pl.BlockSpec((tm, d), lambda i: (i, 0)),
        out_shape=jax.ShapeDtypeStruct((n, d), F32),
        compiler_params=_params("parallel"),
        name="mix_out",
    )(part, z, mk, mv, w_o, x)


def _log_sigmoid(x):
    return jnp.minimum(x, 0.0) - jnp.log1p(jnp.exp(-jnp.abs(x)))


def _cumsum_kernel(*refs, has_prev, t_new):
    if has_prev:
        prev_ref, raw_ref, b_ref, c_ref, lf_ref = refs
    else:
        raw_ref, b_ref, c_ref, lf_ref = refs
    raw = raw_ref[0]
    lf = _log_sigmoid(raw + b_ref[...])
    lf = jnp.where(lax.broadcasted_iota(jnp.int32, lf.shape, 1) < t_new, lf, 0.0)
    lf_ref[0] = lf
    full = jnp.concatenate([prev_ref[0], lf], axis=1) if has_prev else lf
    total = full.shape[1]
    lane = lax.broadcasted_iota(jnp.int32, full.shape, 1)
    shift = 1
    while shift < total:
        full = full + jnp.where(lane >= shift, pltpu.roll(full, shift, 1), 0.0)
        shift *= 2
    c_ref[0] = full


def logf_cumsum(raw_t, b_col, t_new, prev_t=None):
    nb, hp, t_pad = raw_t.shape
    past = 0 if prev_t is None else prev_t.shape[2]
    kern = functools.partial(_cumsum_kernel, has_prev=prev_t is not None, t_new=t_new)
    in_specs = [pl.BlockSpec((1, hp, t_pad), lambda i: (i, 0, 0)), pl.BlockSpec((hp, 1), lambda i: (0, 0))]
    args = [raw_t, b_col]
    if prev_t is not None:
        in_specs = [pl.BlockSpec((1, hp, past), lambda i: (i, 0, 0))] + in_specs
        args = [prev_t] + args
    return pl.pallas_call(
        kern,
        grid=(nb,),
        in_specs=in_specs,
        out_specs=[
            pl.BlockSpec((1, hp, past + t_pad), lambda i: (i, 0, 0)),
            pl.BlockSpec((1, hp, t_pad), lambda i: (i, 0, 0)),
        ],
        out_shape=[jax.ShapeDtypeStruct((nb, hp, past + t_pad), F32), jax.ShapeDtypeStruct((nb, hp, t_pad), F32)],
        compiler_params=_params("parallel"),
        name="logf_cumsum",
    )(*args)


FOX_TILE = 2048
FOX_HEADS_PER_STEP = 2


def _fox_kernel(qi_ref, ki_ref, q_ref, k_ref, v_ref, ck_ref, cq_ref, o_ref, m_ref, l_ref, acc_ref, *, tile):
    step = pl.program_id(2)
    qi = qi_ref[step]
    ki = ki_ref[step]

    @pl.when(ki == 0)
    def _():
        m_ref[...] = jnp.full(m_ref.shape, NEG_INF, F32)
        l_ref[...] = jnp.zeros(l_ref.shape, F32)
        acc_ref[...] = jnp.zeros(acc_ref.shape, F32)

    def block(masked):
        for hh in range(FOX_HEADS_PER_STEP):
            cols = slice(hh * HEAD_DIM, (hh + 1) * HEAD_DIM)
            s = lax.dot_general(q_ref[:, cols], k_ref[:, cols], NT, preferred_element_type=F32) * (HEAD_DIM ** -0.5)
            s = s - (ck_ref[0, hh] - cq_ref[0, hh, :, 0:1])
            if masked:
                qpos = lax.broadcasted_iota(jnp.int32, (tile, tile), 0)
                kpos = lax.broadcasted_iota(jnp.int32, (tile, tile), 1)
                s = jnp.where(kpos <= qpos, s, NEG_INF)
            m_prev = m_ref[hh]
            m_new = jnp.maximum(m_prev, jnp.max(s, axis=-1, keepdims=True))
            alpha = jnp.exp(m_prev - m_new)
            p = jnp.exp(s - m_new)
            l_ref[hh] = alpha * l_ref[hh] + jnp.sum(p, axis=-1, keepdims=True)
            acc_ref[:, cols] = alpha * acc_ref[:, cols] + jnp.dot(p.astype(BF16), v_ref[:, cols],
                                                                  preferred_element_type=F32)
            m_ref[hh] = m_new

    @pl.when(ki < qi)
    def _():
        block(False)

    @pl.when(ki == qi)
    def _():
        block(True)
        for hh in range(FOX_HEADS_PER_STEP):
            cols = slice(hh * HEAD_DIM, (hh + 1) * HEAD_DIM)
            o_ref[:, cols] = (acc_ref[:, cols] / l_ref[hh]).astype(o_ref.dtype)


def fox_prompt(zq, k, v, c, *, batch, seq, heads):
    n = zq.shape[0]
    tile = min(FOX_TILE, seq)
    nt = seq // tile
    hs = FOX_HEADS_PER_STEP
    assert heads % hs == 0
    width = hs * HEAD_DIM
    pairs = [(qi, ki) for qi in range(nt) for ki in range(qi + 1)]
    qi_tab = jnp.asarray([p[0] for p in pairs], jnp.int32)
    ki_tab = jnp.asarray([p[1] for p in pairs], jnp.int32)
    q_map = lambda b, h, s, qt, kt: (b * nt + qt[s], h)
    kv_map = lambda b, h, s, qt, kt: (b * nt + kt[s], h)
    grid_spec = pltpu.PrefetchScalarGridSpec(
        num_scalar_prefetch=2,
        grid=(batch, heads // hs, len(pairs)),
        in_specs=[
            pl.BlockSpec((tile, width), q_map),
            pl.BlockSpec((tile, width), kv_map),
            pl.BlockSpec((tile, width), kv_map),
            pl.BlockSpec((1, hs, 1, tile), lambda b, h, s, qt, kt: (b, h, 0, kt[s])),
            pl.BlockSpec((1, hs, 1, tile), lambda b, h, s, qt, kt: (b, h, 0, qt[s])),
        ],
        out_specs=pl.BlockSpec((tile, width), q_map),
        scratch_shapes=[pltpu.VMEM((hs, tile, 1), F32), pltpu.VMEM((hs, tile, 1), F32),
                        pltpu.VMEM((tile, width), F32)],
    )
    return pl.pallas_call(
        functools.partial(_fox_kernel, tile=tile),
        grid_spec=grid_spec,
        out_shape=jax.ShapeDtypeStruct((n, heads * HEAD_DIM), BF16),
        compiler_params=_params("parallel", "parallel", "arbitrary"),
        name="fox_prompt",
    )(qi_tab, ki_tab, zq, k, v, c, c)


def _fox_sample_kernel(q_ref, kc_ref, vc_ref, kn_ref, vn_ref, c_ref, o_ref, *, past, t_new):
    scale = HEAD_DIM ** -0.5
    q = q_ref[...].astype(BF16)
    c = c_ref[0, 0]
    c_ref0 = c[:, past:past + 1]
    s_old = lax.dot_general(q, kc_ref[0].astype(BF16), NT, preferred_element_type=F32) * scale
    s_old = s_old - (c[:, :past] - c_ref0)
    s_new = lax.dot_general(q, kn_ref[...].astype(BF16), NT, preferred_element_type=F32) * scale
    s_new = s_new - (c[:, past:past + t_new] - c_ref0)
    causal = (lax.broadcasted_iota(jnp.int32, (t_new, t_new), 1)
              <= lax.broadcasted_iota(jnp.int32, (t_new, t_new), 0))
    s_new = jnp.where(causal, s_new, NEG_INF)
    m = jnp.maximum(jnp.max(s_old, axis=-1, keepdims=True), jnp.max(s_new, axis=-1, keepdims=True))
    p_old = jnp.exp(s_old - m)
    p_new = jnp.exp(s_new - m)
    denom = jnp.sum(p_old, axis=-1, keepdims=True) + jnp.sum(p_new, axis=-1, keepdims=True)
    acc = jnp.dot(p_old.astype(BF16), vc_ref[0].astype(BF16), preferred_element_type=F32)
    acc = acc + jnp.dot(p_new.astype(BF16), vn_ref[...].astype(BF16), preferred_element_type=F32)
    o_ref[...] = (acc / denom).astype(o_ref.dtype)


def fox_sample(zq, k_cache, v_cache, k_new, v_new, c, *, batch, t_new, heads):
    past = k_cache.shape[1]
    c_len = c.shape[3]
    kern = functools.partial(_fox_sample_kernel, past=past, t_new=t_new)
    return pl.pallas_call(
        kern,
        grid=(batch, heads),
        in_specs=[
            pl.BlockSpec((t_new, HEAD_DIM), lambda b, h: (b, h)),
            pl.BlockSpec((1, past, HEAD_DIM), lambda b, h: (b, 0, h)),
            pl.BlockSpec((1, past, HEAD_DIM), lambda b, h: (b, 0, h)),
            pl.BlockSpec((t_new, HEAD_DIM), lambda b, h: (b, h)),
            pl.BlockSpec((t_new, HEAD_DIM), lambda b, h: (b, h)),
            pl.BlockSpec((1, 1, 1, c_len), lambda b, h: (b, h, 0, 0)),
        ],
        out_specs=pl.BlockSpec((t_new, HEAD_DIM), lambda b, h: (b, h)),
        out_shape=jax.ShapeDtypeStruct((batch * t_new, heads * HEAD_DIM), BF16),
        compiler_params=_params("parallel", "parallel"),
        name="fox_sample",
    )(zq, k_cache, v_cache, k_new, v_new, c)


def _top_rows(vals_list, count, payloads=None):
    n = len(vals_list)
    rows = [lax.broadcasted_iota(jnp.int32, v.shape, 0) for v in vals_list]
    vals = list(vals_list)
    out_v = [[] for _ in range(n)]
    out_i = [[] for _ in range(n)]
    for _ in range(count):
        for p in range(n):
            m = jnp.max(vals[p], axis=0, keepdims=True)
            idx = jnp.min(jnp.where(vals[p] == m, rows[p], vals[p].shape[0]), axis=0, keepdims=True)
            hit = rows[p] == idx
            out_v[p].append(m)
            out_i[p].append(idx if payloads is None
                            else jnp.max(jnp.where(hit, payloads[p], -1), axis=0, keepdims=True))
            vals[p] = jnp.where(hit, -jnp.inf, vals[p])
    return out_v, out_i


ROUTE_HEADS_PER_STEP = 4


def _peer_route_kernel(q_ref, keys_ref, eidx_ref, gate_ref, e_scr, g_scr):
    step = pl.program_id(1)
    q = q_ref[...].astype(BF16)
    hs = ROUTE_HEADS_PER_STEP
    scores = [lax.dot_general(keys_ref[hh, p].astype(BF16), q[:, (2 * hh + p) * LANES:(2 * hh + p + 1) * LANES], NT,
                              preferred_element_type=F32) for hh in range(hs) for p in range(2)]
    top_v, top_i = _top_rows(scores, PEER_TOPK)
    half = PEER_TOPK // 2
    sub = lax.broadcasted_iota(jnp.int32, (half, q.shape[0]), 0)
    cands, cidxs = [], []
    for hh in range(hs):
        hv0, hi0, hv1, hi1 = top_v[2 * hh], top_i[2 * hh], top_v[2 * hh + 1], top_i[2 * hh + 1]
        v0 = jnp.concatenate(hv0, axis=0)
        i0 = jnp.concatenate(hi0, axis=0)
        v1 = jnp.concatenate(hv1, axis=0)
        i1 = jnp.concatenate(hi1, axis=0)
        cand = [hv0[0] + v1]
        cidx = [hi0[0] * PEER_NKEYS + i1]
        for a in range(1, half):
            keep = sub < PEER_TOPK // (a + 1)
            cand.append(jnp.where(keep, hv0[a] + v1[:half], -jnp.inf))
            cidx.append(hi0[a] * PEER_NKEYS + i1[:half])
        cand.append(v0[half:] + hv1[0])
        cidx.append(i0[half:] * PEER_NKEYS + hi1[0])
        cands.append(jnp.concatenate(cand, axis=0))
        cidxs.append(jnp.concatenate(cidx, axis=0))
    bests, eidxs = _top_rows(cands, PEER_TOPK, payloads=cidxs)
    for hh in range(hs):
        best = jnp.concatenate(bests[hh], axis=0)
        e = jnp.exp(best - jnp.max(best, axis=0, keepdims=True))
        gate = e / jnp.sum(e, axis=0, keepdims=True)
        row0 = pl.multiple_of((step * hs + hh) * PEER_TOPK, PEER_TOPK)
        e_scr[pl.ds(row0, PEER_TOPK), :] = jnp.concatenate(eidxs[hh], axis=0)
        g_scr[pl.ds(row0, PEER_TOPK), :] = gate

    @pl.when(step == pl.num_programs(1) - 1)
    def _():
        eidx_ref[...] = e_scr[...].T
        gate_ref[...] = g_scr[...].T


def peer_route(q, sub_keys):
    n = q.shape[0]
    tt = min(256, n)
    hs = ROUTE_HEADS_PER_STEP
    assert n % tt == 0 and PEER_HEADS % hs == 0
    return pl.pallas_call(
        _peer_route_kernel,
        grid=(n // tt, PEER_HEADS // hs),
        in_specs=[
            pl.BlockSpec((tt, 2 * hs * LANES), lambda i, h: (i, h)),
            pl.BlockSpec((hs, 2, PEER_NKEYS, LANES), lambda i, h: (h, 0, 0, 0)),
        ],
        out_specs=[
            pl.BlockSpec((tt, PEER_PAIRS), lambda i, h: (i, 0)),
            pl.BlockSpec((tt, PEER_PAIRS), lambda i, h: (i, 0)),
        ],
        out_shape=[jax.ShapeDtypeStruct((n, PEER_PAIRS), jnp.int32), jax.ShapeDtypeStruct((n, PEER_PAIRS), F32)],
        scratch_shapes=[pltpu.VMEM((PEER_PAIRS, tt), jnp.int32), pltpu.VMEM((PEER_PAIRS, tt), F32)],
        compiler_params=_params("parallel", "arbitrary"),
        name="peer_route",
    )(q, sub_keys)


PEER_SPLIT = 8
PEER_CHUNK = PEER_SPLIT * LANES
PEER_ROWS = PEER_PAIRS * PEER_SPLIT
PACKED_ROWS = PEER_SPLIT // 2


def _peer_pass_kernel(*refs, mode, group, chunk):
    if mode == "u_first":
        idx_ref, h_ref, slab_hbm, o_ref, slab, stage, tiles, sem = refs
    elif mode == "u_last":
        idx_ref, h_ref, w_ref, gate_ref, slab_hbm, o_ref, slab, stage, tiles, sem = refs
    else:
        idx_ref, coef_ref, x_ref, slab_hbm, o_ref, slab, stage, tiles, sem = refs

    @pl.when(pl.program_id(0) == 0)
    def _():
        load = pltpu.make_async_copy(slab_hbm.at[chunk], slab, sem.at[0])
        load.start()
        load.wait()

    if mode != "v":
        for r in range(PEER_SPLIT):
            tiles[pl.ds(r, group, stride=PEER_SPLIT), :] = h_ref[:, r * LANES:(r + 1) * LANES]

    lane = lax.broadcasted_iota(jnp.int32, (PEER_SPLIT, PEER_ROWS), 1)
    diag = (lane % PEER_SPLIT) == lax.broadcasted_iota(jnp.int32, (PEER_SPLIT, PEER_ROWS), 0)

    def one_token(t, j):
        for k in range(PEER_PAIRS):
            row = pl.multiple_of(idx_ref[t, k], PACKED_ROWS)
            stage[j, pl.ds(k * PACKED_ROWS, PACKED_ROWS), :] = slab[pl.ds(row, PACKED_ROWS), :]
        m16 = pltpu.bitcast(stage[j], BF16)
        r0 = pl.multiple_of(t * PEER_SPLIT, PEER_SPLIT)
        if mode == "v":
            coef = coef_ref[pl.ds(t, 1), :]
            lhs = jnp.where(diag, jnp.broadcast_to(coef, (PEER_SPLIT, PEER_ROWS)), 0.0).astype(BF16)
            tiles[pl.ds(r0, PEER_SPLIT), :] = jnp.dot(lhs, m16, preferred_element_type=F32)
            return
        h16 = tiles[pl.ds(r0, PEER_SPLIT), :].astype(BF16)
        yt = lax.dot_general(h16, m16, NT, preferred_element_type=F32)
        o_ref[pl.ds(t, 1), :] = jnp.sum(jnp.where(diag, yt, 0.0), axis=0, keepdims=True)

    def body(i, carry):
        for j in range(PEER_UNROLL):
            one_token(i * PEER_UNROLL + j, j)
        return carry

    lax.fori_loop(0, group // PEER_UNROLL, body, 0)
    if mode == "v":
        lo, hi = chunk * PEER_CHUNK, (chunk + 1) * PEER_CHUNK
        if lo > 0:
            o_ref[:, :lo] = x_ref[:, :lo]
        if hi < x_ref.shape[1]:
            o_ref[:, hi:] = x_ref[:, hi:]
        for r in range(PEER_SPLIT):
            cols = slice(lo + r * LANES, lo + (r + 1) * LANES)
            o_ref[:, cols] = x_ref[:, cols] + tiles[pl.ds(r, group, stride=PEER_SPLIT), :]
    if mode == "u_last":
        w = o_ref[...] + w_ref[...]
        lane_g = lax.broadcasted_iota(jnp.int32, w.shape, 1)
        for s in (1, 2, 4):
            up = pltpu.roll(w, s, 1)
            down = pltpu.roll(w, PEER_ROWS - s, 1)
            w = w + jnp.where((lane_g & s) != 0, up, down)
        o_ref[...] = gate_ref[...] * jax.nn.gelu(w)


def peer_pass(mode, idx, slabs, *operands, group=64):
    n = idx.shape[0]
    group = min(group, n)
    assert n % group == 0 and group % PEER_UNROLL == 0
    vec = pl.BlockSpec((group, PEER_ROWS), lambda i: (i, 0))
    idx_spec = pl.BlockSpec((group, PEER_PAIRS), lambda i: (i, 0), memory_space=pltpu.SMEM)
    any_spec = pl.BlockSpec(memory_space=pl.ANY)
    chunk = operands[-1]
    cols = pl.BlockSpec((group, PEER_CHUNK), lambda i: (i, chunk))
    if mode == "v":
        coef, x = operands[:2]
        full = pl.BlockSpec((group, x.shape[1]), lambda i: (i, 0))
        in_specs, args = [idx_spec, vec, full, any_spec], (idx, coef, x, slabs)
        out_spec, out_shape = full, jax.ShapeDtypeStruct(x.shape, F32)
    else:
        h = operands[0]
        extra = list(operands[1:-1])
        in_specs, args = [idx_spec, cols] + [vec] * len(extra) + [any_spec], (idx, h, *extra, slabs)
        out_spec, out_shape = vec, jax.ShapeDtypeStruct((n, PEER_ROWS), F32)
    return pl.pallas_call(
        functools.partial(_peer_pass_kernel, mode=mode, group=group, chunk=chunk),
        grid=(n // group,),
        in_specs=in_specs,
        out_specs=out_spec,
        out_shape=out_shape,
        scratch_shapes=[
            pltpu.VMEM(slabs.shape[1:], jnp.uint32),
            pltpu.VMEM((PEER_UNROLL, PEER_PAIRS * PACKED_ROWS, LANES), jnp.uint32),
            pltpu.VMEM((group * PEER_SPLIT, LANES), F32),
            pltpu.SemaphoreType.DMA((1,)),
        ],
        compiler_params=_params("arbitrary"),
        name="peer_" + mode,
    )(*args)


def _pack_table_kernel(t_ref, o_ref, *, rows):
    x = t_ref[0]
    for c in range(o_ref.shape[0]):
        for j in range(PACKED_ROWS):
            col = c * PEER_CHUNK + 2 * j * LANES
            lo = pltpu.bitcast(x[:, col:col + LANES].astype(BF16).astype(F32), jnp.uint32)
            hi = pltpu.bitcast(x[:, col + LANES:col + 2 * LANES].astype(BF16).astype(F32), jnp.uint32)
            o_ref[c, pl.ds(j, rows, stride=PACKED_ROWS), :] = (lo >> 16) | (hi & jnp.uint32(0xFFFF0000))


def pack_table(tables, layer):
    _, e, d = tables.shape
    rows = min(512, e)
    assert e % rows == 0 and d % PEER_CHUNK == 0
    n_chunks = d // PEER_CHUNK
    return pl.pallas_call(
        functools.partial(_pack_table_kernel, rows=rows),
        grid=(e // rows,),
        in_specs=[pl.BlockSpec((1, rows, d), lambda i: (layer, i, 0))],
        out_specs=pl.BlockSpec((n_chunks, rows * PACKED_ROWS, LANES), lambda i: (0, i, 0)),
        out_shape=jax.ShapeDtypeStruct((n_chunks, e * PACKED_ROWS, LANES), jnp.uint32),
        compiler_params=_params("parallel"),
        name="pack_table",
    )(tables)


def peer_apply(x, h, eidx, gate, tables):
    u_slabs, v_slabs = tables
    n_chunks = u_slabs.shape[0]
    assert n_chunks == 2 and v_slabs.shape[0] == n_chunks
    idx = eidx * PACKED_ROWS
    gate8 = jnp.repeat(gate, PEER_SPLIT, axis=1)
    w = peer_pass("u_first", idx, u_slabs, h, 0)
    w = peer_pass("u_last", idx, u_slabs, h, w, gate8, 1)
    y = x
    for c in range(n_chunks):
        y = peer_pass("v", idx, v_slabs, w, y, c)
    return y


def _rms_kernel(x_ref, g_ref, o_ref):
    o_ref[...] = _rms(x_ref[...], g_ref[...])


def rms_norm(x, g):
    n, d = x.shape
    tm = min(512, n)
    return pl.pallas_call(
        _rms_kernel,
        grid=(n // tm,),
        in_specs=[pl.BlockSpec((tm, d), lambda i: (i, 0)), pl.BlockSpec((1, d), lambda i: (0, 0))],
        out_specs=pl.BlockSpec((tm, d), lambda i: (i, 0)),
        out_shape=jax.ShapeDtypeStruct((n, d), F32),
        compiler_params=_params("parallel"),
        name="rms_norm",
    )(x, g.reshape(1, d))


def peer_ffn(x, g, w_pq16, sub_keys, tables, g_final=None):
    q, h = norm_matmul(x, g, w_pq16, emit_norm=True)
    eidx, gate = peer_route(q, sub_keys)
    y = peer_apply(x, h, eidx, gate, tables)
    return y if g_final is None else rms_norm(y, g_final)


def _forget_inputs(zf, batch, t_new, t_pad):
    raw = zf[:, :FOX_HEADS_PAD].reshape(batch, t_new, FOX_HEADS_PAD)
    raw = jnp.swapaxes(raw, 1, 2)
    return jnp.pad(raw, ((0, 0), (0, 0), (0, t_pad - t_new)))


def kernel(x_prompt, x_sample, mem_prompt, cache_mem_k, cache_mem_v, cache_fox_k, cache_fox_v, cache_fox_logf, g_mix, g_ffn, w_o, g_mem, w_mem_kv, w_pq, peer_sub_keys, peer_u, peer_v, w_in_a, gmlp_ws, gmlp_b, gmlp_gv, w_in_b, g_kv, w_kv, b_f, g_final):
    batch, seq, d = x_prompt.shape
    dec_batch, dec_seq, _ = x_sample.shape
    depth = g_mix.shape[0]
    n_a = w_in_a.shape[0]
    n_mem = mem_prompt.shape[1]
    past = cache_fox_k.shape[1]
    heads = cache_fox_k.shape[2]
    fox_width = heads * HEAD_DIM
    gmlp_width = gmlp_gv.shape[1]
    assert gmlp_width == fox_width and d == fox_width + MEM_WIDTH

    w_o16 = w_o.astype(BF16)
    w_pq16 = w_pq.astype(BF16)
    w_in_a16 = w_in_a.astype(BF16)
    w_in_b16 = w_in_b.astype(BF16)
    w_mem16 = w_mem_kv.astype(BF16)
    w_k16 = w_kv[:, :fox_width].astype(BF16)
    w_v16 = w_kv[:, fox_width:2 * fox_width].astype(BF16)
    w_f16 = jnp.pad(w_kv[:, 2 * fox_width:], ((0, 0), (0, LANES - heads))).astype(BF16)
    b_col = jnp.pad(b_f, (0, FOX_HEADS_PAD - heads)).reshape(FOX_HEADS_PAD, 1)
    uv = [(pack_table(peer_u, l), pack_table(peer_v, l)) for l in range(depth)]

    def ffn(x, l, final=False):
        return peer_ffn(x, g_ffn[l], w_pq16[l], peer_sub_keys[l], uv[l], g_final if final else None)

    def shared_kv(x):
        k, k16 = norm_matmul(x, g_kv, w_k16, emit_bf16=True)
        v, v16 = norm_matmul(x, g_kv, w_v16, emit_bf16=True)
        zf = norm_matmul(x, g_kv, w_f16)
        return k, v, zf, k16, v16

    def run_group(x3, mem_k, mem_v, chunk, fox_fn):
        b, t, _ = x3.shape
        x = x3.reshape(b * t, d)
        gmlp_v = []
        extras = None
        for l in range(depth):
            if l < n_a:
                z = norm_matmul(x, g_mix[l], w_in_a16[l])
                part, v_rows = gmlp_mix(z, gmlp_ws[l], gmlp_b[l], gmlp_gv[l], chunk=chunk, width=gmlp_width)
                gmlp_v.append(v_rows.reshape(b, t, gmlp_width))
                x = mix_out(part, z, 2 * gmlp_width // MEM_WIDTH, mem_k[l], mem_v[l], w_o16[l], x, seq=t)
                x = ffn(x, l)
                if l == n_a - 1:
                    kvf = shared_kv(x)
            else:
                z = norm_matmul(x, g_mix[l], w_in_b16[l - n_a], out_dtype=BF16)
                part, extras = fox_fn(z, kvf, extras)
                x = mix_out(part, z, fox_width // MEM_WIDTH, mem_k[l], mem_v[l], w_o16[l], x, seq=t)
                x = ffn(x, l, final=(l == depth - 1))
        return x.reshape(b, t, d), gmlp_v, extras

    mem2d = mem_prompt.reshape(batch * n_mem, d)
    mem_k_p, mem_v_p = [], []
    for l in range(depth):
        zkv = norm_matmul(mem2d, g_mem[l], w_mem16[l])
        mem_k_p.append(zkv[:, :MEM_WIDTH].reshape(batch, n_mem, MEM_WIDTH))
        mem_v_p.append(zkv[:, MEM_WIDTH:].reshape(batch, n_mem, MEM_WIDTH))

    def fox_p(z, kvf, extras):
        if extras is None:
            k, v, zf, k16, v16 = kvf
            c, lf = logf_cumsum(_forget_inputs(zf, batch, seq, seq), b_col, seq)
            extras = (k, v, lf, c[:, :heads].reshape(batch, heads, 1, seq), k16, v16)
        k, v, lf, c, k16, v16 = extras
        return fox_prompt(z, k16, v16, c, batch=batch, seq=seq, heads=heads), extras

    y_p, _, ex_p = run_group(x_prompt, [a.astype(BF16) for a in mem_k_p], [a.astype(BF16) for a in mem_v_p],
                             min(seq, GMLP_CHUNK), fox_p)
    fox_k_p = ex_p[0].reshape(batch, seq, heads, HEAD_DIM)
    fox_v_p = ex_p[1].reshape(batch, seq, heads, HEAD_DIM)
    fox_logf_p = jnp.swapaxes(ex_p[2][:, :heads, :seq], 1, 2)
    new_mem_k = jnp.stack(mem_k_p).reshape(depth, batch, n_mem, MEM_HEADS, HEAD_DIM)
    new_mem_v = jnp.stack(mem_v_p).reshape(depth, batch, n_mem, MEM_HEADS, HEAD_DIM)

    t_pad = -(-dec_seq // LANES) * LANES
    prev_t = jnp.pad(jnp.swapaxes(cache_fox_logf, 1, 2), ((0, 0), (0, FOX_HEADS_PAD - heads), (0, 0)))
    k_cache = cache_fox_k.reshape(dec_batch, past, fox_width)
    v_cache = cache_fox_v.reshape(dec_batch, past, fox_width)

    def fox_s(z, kvf, extras):
        if extras is None:
            k, v, zf = kvf[:3]
            c, lf = logf_cumsum(_forget_inputs(zf, dec_batch, dec_seq, t_pad), b_col, dec_seq, prev_t)
            extras = (k, v, lf, c[:, :heads].reshape(dec_batch, heads, 1, past + t_pad))
        k, v, lf, c = extras
        return fox_sample(z, k_cache, v_cache, k, v, c, batch=dec_batch, t_new=dec_seq, heads=heads), extras

    mem_k_s = [cache_mem_k[l].reshape(dec_batch, n_mem, MEM_WIDTH).astype(BF16) for l in range(depth)]
    mem_v_s = [cache_mem_v[l].reshape(dec_batch, n_mem, MEM_WIDTH).astype(BF16) for l in range(depth)]
    y_s, gmlp_v_s, ex_s = run_group(x_sample, mem_k_s, mem_v_s, min(dec_seq, GMLP_CHUNK), fox_s)
    fox_k_s = ex_s[0].reshape(dec_batch, dec_seq, heads, HEAD_DIM)
    fox_v_s = ex_s[1].reshape(dec_batch, dec_seq, heads, HEAD_DIM)
    fox_logf_s = jnp.swapaxes(ex_s[2][:, :heads, :dec_seq], 1, 2)

    return (y_p, y_s, new_mem_k, new_mem_v, fox_k_p, fox_v_p, fox_logf_p,
            fox_k_s, fox_v_s, fox_logf_s, jnp.stack(gmlp_v_s))
```

```python
import functools

import jax
import jax.numpy as jnp
from jax import lax
from jax.experimental import pallas as pl
from jax.experimental.pallas import tpu as pltpu

EPS = 1e-6
NEG_INF = -1e30
LANES = 128
HEAD_DIM = 128
MEM_HEADS = 4
MEM_WIDTH = MEM_HEADS * HEAD_DIM
GMLP_GROUPS = 4
GMLP_CHUNK = 128
PEER_HEADS = 8
PEER_TOPK = 16
PEER_NKEYS = 128
PEER_PAIRS = PEER_HEADS * PEER_TOPK
PEER_UNROLL = 32
FOX_HEADS_PAD = 16
VMEM_LIMIT = 56 * 1024 * 1024
BF16 = jnp.bfloat16
F32 = jnp.float32
NT = (((1,), (1,)), ((), ()))


def _params(*sem):
    return pltpu.CompilerParams(dimension_semantics=sem, vmem_limit_bytes=VMEM_LIMIT)


def _rms(x, g):
    return x * lax.rsqrt(jnp.mean(x * x, axis=-1, keepdims=True) + EPS) * g


NORM_MATMUL_WEIGHT_BYTES = 16 * 1024 * 1024


def _norm_matmul_kernel(x_ref, g_ref, w_ref, o_ref, *rest, emit_norm, emit_bf16, split_heads):
    xn_ref = rest[-1]

    @pl.when(pl.program_id(1) == 0)
    def _():
        h = _rms(x_ref[...], g_ref[...])
        xn_ref[...] = h.astype(BF16)
        if emit_norm:
            rest[0][...] = h

    y = jnp.dot(xn_ref[...], w_ref[...], preferred_element_type=F32)
    if split_heads:
        for h in range(o_ref.shape[1]):
            o_ref[:, h, :] = y[:, h * HEAD_DIM:(h + 1) * HEAD_DIM].astype(o_ref.dtype)
    else:
        o_ref[...] = y.astype(o_ref.dtype)
    if emit_bf16:
        rest[-2][...] = y.astype(BF16)


def norm_matmul(x, g, w, *, out_dtype=F32, emit_norm=False, emit_bf16=False, split_heads=False):
    n, d = x.shape
    dout = w.shape[1]
    tm = min(512, n)
    tn = dout if d * dout * 2 <= NORM_MATMUL_WEIGHT_BYTES else 512
    assert n % tm == 0 and dout % tn == 0
    if split_heads:
        assert tn == dout and dout % HEAD_DIM == 0
        out_specs = [pl.BlockSpec((tm, dout // HEAD_DIM, HEAD_DIM), lambda i, j: (i, 0, 0))]
        out_shape = [jax.ShapeDtypeStruct((n, dout // HEAD_DIM, HEAD_DIM), out_dtype)]
    else:
        out_specs = [pl.BlockSpec((tm, tn), lambda i, j: (i, j))]
        out_shape = [jax.ShapeDtypeStruct((n, dout), out_dtype)]
    if emit_norm:
        out_specs.append(pl.BlockSpec((tm, d), lambda i, j: (i, 0)))
        out_shape.append(jax.ShapeDtypeStruct((n, d), F32))
    if emit_bf16:
        out_specs.append(pl.BlockSpec((tm, tn), lambda i, j: (i, j)))
        out_shape.append(jax.ShapeDtypeStruct((n, dout), BF16))
    out = pl.pallas_call(
        functools.partial(_norm_matmul_kernel, emit_norm=emit_norm, emit_bf16=emit_bf16, split_heads=split_heads),
        grid=(n // tm, dout // tn),
        in_specs=[
            pl.BlockSpec((tm, d), lambda i, j: (i, 0)),
            pl.BlockSpec((1, d), lambda i, j: (0, 0)),
            pl.BlockSpec((d, tn), lambda i, j: (0, j)),
        ],
        out_specs=out_specs,
        out_shape=out_shape,
        scratch_shapes=[pltpu.VMEM((tm, d), BF16)],
        compiler_params=_params("parallel", "arbitrary"),
        name="norm_matmul",
    )(x, g.reshape(1, d), w)
    return out[0] if len(out) == 1 else tuple(out)


def _gmlp_kernel(zu_ref, zv_ref, w_ref, b_ref, gv_ref, mix_ref, v_ref, *, chunk, n_chunks, gdim):
    tril = (lax.broadcasted_iota(jnp.int32, (chunk, chunk), 1)
            <= lax.broadcasted_iota(jnp.int32, (chunk, chunk), 0))
    for c in range(n_chunks):
        rows = slice(c * chunk, (c + 1) * chunk)
        u = jax.nn.gelu(zu_ref[rows, :])
        v = _rms(jax.nn.gelu(zv_ref[rows, :]), gv_ref[...])
        v_ref[rows, :] = v
        v16 = v.astype(BF16)
        for g in range(GMLP_GROUPS):
            cols = slice(g * gdim, (g + 1) * gdim)
            w = jnp.where(tril, w_ref[g], 0.0).astype(BF16)
            mixed = jnp.dot(w, v16[:, cols], preferred_element_type=F32) + b_ref[g]
            mix_ref[rows, cols] = (u[:, cols] * mixed).astype(BF16)


def gmlp_mix(z, w_s, b_s, g_v, *, chunk, width):
    n = z.shape[0]
    tm = min(512, n)
    assert n % tm == 0 and tm % chunk == 0
    gdim = width // GMLP_GROUPS
    w = w_s[:, :chunk, :chunk]
    b = b_s[:, :chunk].reshape(GMLP_GROUPS, chunk, 1)
    kern = functools.partial(_gmlp_kernel, chunk=chunk, n_chunks=tm // chunk, gdim=gdim)
    return pl.pallas_call(
        kern,
        grid=(n // tm,),
        in_specs=[
            pl.BlockSpec((tm, width), lambda i: (i, 0)),
            pl.BlockSpec((tm, width), lambda i: (i, 1)),
            pl.BlockSpec((GMLP_GROUPS, chunk, chunk), lambda i: (0, 0, 0)),
            pl.BlockSpec((GMLP_GROUPS, chunk, 1), lambda i: (0, 0, 0)),
            pl.BlockSpec((1, width), lambda i: (0, 0)),
        ],
        out_specs=[
            pl.BlockSpec((tm, width), lambda i: (i, 0)),
            pl.BlockSpec((tm, width), lambda i: (i, 0)),
        ],
        out_shape=[jax.ShapeDtypeStruct((n, width), BF16), jax.ShapeDtypeStruct((n, width), F32)],
        compiler_params=_params("parallel"),
        name="gmlp_mix",
    )(z, z, w, b, g_v.reshape(1, width))


def _mix_out_kernel(part_ref, qm_ref, mk_ref, mv_ref, wo_ref, x_ref, o_ref, *, n_sub, t_sub, width):
    scale = HEAD_DIM ** -0.5
    subs = []
    for bi in range(n_sub):
        rows = slice(bi * t_sub, (bi + 1) * t_sub)
        heads = []
        for h in range(MEM_HEADS):
            cols = slice(h * HEAD_DIM, (h + 1) * HEAD_DIM)
            q = qm_ref[rows, cols].astype(BF16)
            s = lax.dot_general(q, mk_ref[bi, :, cols], NT, preferred_element_type=F32) * scale
            s = s - jnp.max(s, axis=-1, keepdims=True)
            p = jnp.exp(s)
            p = (p / jnp.sum(p, axis=-1, keepdims=True)).astype(BF16)
            heads.append(jnp.dot(p, mv_ref[bi, :, cols], preferred_element_type=F32))
        subs.append(jnp.concatenate(heads, axis=1))
    mem = (subs[0] if n_sub == 1 else jnp.concatenate(subs, axis=0)).astype(BF16)
    y = jnp.dot(part_ref[...], wo_ref[:width, :], preferred_element_type=F32)
    y = y + jnp.dot(mem, wo_ref[width:, :], preferred_element_type=F32)
    o_ref[...] = x_ref[...] + y


def mix_out(part, z, q_block, mk, mv, w_o, x, *, seq):
    n, d = x.shape
    width = part.shape[1]
    n_mem = mk.shape[1]
    tm = min(256, n)
    assert n % tm == 0
    if seq >= tm:
        assert seq % tm == 0
        n_sub, t_sub = 1, tm
        kv_map = lambda i: ((i * tm) // seq, 0, 0)
    else:
        assert tm % seq == 0
        n_sub, t_sub = tm // seq, seq
        kv_map = lambda i: (i, 0, 0)
    kern = functools.partial(_mix_out_kernel, n_sub=n_sub, t_sub=t_sub, width=width)
    return pl.pallas_call(
        kern,
        grid=(n // tm,),
        in_specs=[
            pl.BlockSpec((tm, width), lambda i: (i, 0)),
            pl.BlockSpec((tm, MEM_WIDTH), lambda i: (i, q_block)),
            pl.BlockSpec((n_sub, n_mem, MEM_WIDTH), kv_map),
            pl.BlockSpec((n_sub, n_mem, MEM_WIDTH), kv_map),
            pl.BlockSpec((width + MEM_WIDTH, d), lambda i: (0, 0)),
            pl.BlockSpec((tm, d), lambda i: (i, 0)),
        ],
        out_specs=pl.BlockSpec((tm, d), lambda i: (i, 0)),
        out_shape=jax.ShapeDtypeStruct((n, d), F32),
        compiler_params=_params("parallel"),
        name="mix_out",
    )(part, z, mk, mv, w_o, x)


def _log_sigmoid(x):
    return jnp.minimum(x, 0.0) - jnp.log1p(jnp.exp(-jnp.abs(x)))


def _cumsum_kernel(*refs, has_prev, t_new):
    if has_prev:
        prev_ref, raw_ref, b_ref, c_ref, lf_ref = refs
    else:
        raw_ref, b_ref, c_ref, lf_ref = refs
    raw = raw_ref[0]
    lf = _log_sigmoid(raw + b_ref[...])
    lf = jnp.where(lax.broadcasted_iota(jnp.int32, lf.shape, 1) < t_new, lf, 0.0)
    lf_ref[0] = lf
    full = jnp.concatenate([prev_ref[0], lf], axis=1) if has_prev else lf
    total = full.shape[1]
    lane = lax.broadcasted_iota(jnp.int32, full.shape, 1)
    shift = 1
    while shift < total:
        full = full + jnp.where(lane >= shift, pltpu.roll(full, shift, 1), 0.0)
        shift *= 2
    c_ref[0] = full


def logf_cumsum(raw_t, b_col, t_new, prev_t=None):
    nb, hp, t_pad = raw_t.shape
    past = 0 if prev_t is None else prev_t.shape[2]
    kern = functools.partial(_cumsum_kernel, has_prev=prev_t is not None, t_new=t_new)
    in_specs = [pl.BlockSpec((1, hp, t_pad), lambda i: (i, 0, 0)), pl.BlockSpec((hp, 1), lambda i: (0, 0))]
    args = [raw_t, b_col]
    if prev_t is not None:
        in_specs = [pl.BlockSpec((1, hp, past), lambda i: (i, 0, 0))] + in_specs
        args = [prev_t] + args
    return pl.pallas_call(
        kern,
        grid=(nb,),
        in_specs=in_specs,
        out_specs=[
            pl.BlockSpec((1, hp, past + t_pad), lambda i: (i, 0, 0)),
            pl.BlockSpec((1, hp, t_pad), lambda i: (i, 0, 0)),
        ],
        out_shape=[jax.ShapeDtypeStruct((nb, hp, past + t_pad), F32), jax.ShapeDtypeStruct((nb, hp, t_pad), F32)],
        compiler_params=_params("parallel"),
        name="logf_cumsum",
    )(*args)


FOX_TILE = 2048
FOX_HEADS_PER_STEP = 2


def _fox_kernel(qi_ref, ki_ref, q_ref, k_ref, v_ref, ck_ref, cq_ref, o_ref, m_ref, l_ref, acc_ref, *, tile):
    step = pl.program_id(2)
    qi = qi_ref[step]
    ki = ki_ref[step]

    @pl.when(ki == 0)
    def _():
        m_ref[...] = jnp.full(m_ref.shape, NEG_INF, F32)
        l_ref[...] = jnp.zeros(l_ref.shape, F32)
        acc_ref[...] = jnp.zeros(acc_ref.shape, F32)

    def block(masked):
        for hh in range(FOX_HEADS_PER_STEP):
            cols = slice(hh * HEAD_DIM, (hh + 1) * HEAD_DIM)
            s = lax.dot_general(q_ref[:, cols], k_ref[:, cols], NT, preferred_element_type=F32) * (HEAD_DIM ** -0.5)
            s = s - (ck_ref[0, hh] - cq_ref[0, hh, :, 0:1])
            if masked:
                qpos = lax.broadcasted_iota(jnp.int32, (tile, tile), 0)
                kpos = lax.broadcasted_iota(jnp.int32, (tile, tile), 1)
                s = jnp.where(kpos <= qpos, s, NEG_INF)
            m_prev = m_ref[hh]
            m_new = jnp.maximum(m_prev, jnp.max(s, axis=-1, keepdims=True))
            alpha = jnp.exp(m_prev - m_new)
            p = jnp.exp(s - m_new)
            l_ref[hh] = alpha * l_ref[hh] + jnp.sum(p, axis=-1, keepdims=True)
            acc_ref[:, cols] = alpha * acc_ref[:, cols] + jnp.dot(p.astype(BF16), v_ref[:, cols],
                                                                  preferred_element_type=F32)
            m_ref[hh] = m_new

    @pl.when(ki < qi)
    def _():
        block(False)

    @pl.when(ki == qi)
    def _():
        block(True)
        for hh in range(FOX_HEADS_PER_STEP):
            cols = slice(hh * HEAD_DIM, (hh + 1) * HEAD_DIM)
            o_ref[:, cols] = (acc_ref[:, cols] / l_ref[hh]).astype(o_ref.dtype)


def fox_prompt(zq, k, v, c, *, batch, seq, heads):
    n = zq.shape[0]
    tile = min(FOX_TILE, seq)
    nt = seq // tile
    hs = FOX_HEADS_PER_STEP
    assert heads % hs == 0
    width = hs * HEAD_DIM
    pairs = [(qi, ki) for qi in range(nt) for ki in range(qi + 1)]
    qi_tab = jnp.asarray([p[0] for p in pairs], jnp.int32)
    ki_tab = jnp.asarray([p[1] for p in pairs], jnp.int32)
    q_map = lambda b, h, s, qt, kt: (b * nt + qt[s], h)
    kv_map = lambda b, h, s, qt, kt: (b * nt + kt[s], h)
    grid_spec = pltpu.PrefetchScalarGridSpec(
        num_scalar_prefetch=2,
        grid=(batch, heads // hs, len(pairs)),
        in_specs=[
            pl.BlockSpec((tile, width), q_map),
            pl.BlockSpec((tile, width), kv_map),
            pl.BlockSpec((tile, width), kv_map),
            pl.BlockSpec((1, hs, 1, tile), lambda b, h, s, qt, kt: (b, h, 0, kt[s])),
            pl.BlockSpec((1, hs, 1, tile), lambda b, h, s, qt, kt: (b, h, 0, qt[s])),
        ],
        out_specs=pl.BlockSpec((tile, width), q_map),
        scratch_shapes=[pltpu.VMEM((hs, tile, 1), F32), pltpu.VMEM((hs, tile, 1), F32),
                        pltpu.VMEM((tile, width), F32)],
    )
    return pl.pallas_call(
        functools.partial(_fox_kernel, tile=tile),
        grid_spec=grid_spec,
        out_shape=jax.ShapeDtypeStruct((n, heads * HEAD_DIM), BF16),
        compiler_params=_params("parallel", "parallel", "arbitrary"),
        name="fox_prompt",
    )(qi_tab, ki_tab, zq, k, v, c, c)


def _fox_sample_kernel(q_ref, kc_ref, vc_ref, kn_ref, vn_ref, c_ref, o_ref, *, past, t_new):
    scale = HEAD_DIM ** -0.5
    q = q_ref[...].astype(BF16)
    c = c_ref[0, 0]
    c_ref0 = c[:, past:past + 1]
    s_old = lax.dot_general(q, kc_ref[0].astype(BF16), NT, preferred_element_type=F32) * scale
    s_old = s_old - (c[:, :past] - c_ref0)
    s_new = lax.dot_general(q, kn_ref[...].astype(BF16), NT, preferred_element_type=F32) * scale
    s_new = s_new - (c[:, past:past + t_new] - c_ref0)
    causal = (lax.broadcasted_iota(jnp.int32, (t_new, t_new), 1)
              <= lax.broadcasted_iota(jnp.int32, (t_new, t_new), 0))
    s_new = jnp.where(causal, s_new, NEG_INF)
    m = jnp.maximum(jnp.max(s_old, axis=-1, keepdims=True), jnp.max(s_new, axis=-1, keepdims=True))
    p_old = jnp.exp(s_old - m)
    p_new = jnp.exp(s_new - m)
    denom = jnp.sum(p_old, axis=-1, keepdims=True) + jnp.sum(p_new, axis=-1, keepdims=True)
    acc = jnp.dot(p_old.astype(BF16), vc_ref[0].astype(BF16), preferred_element_type=F32)
    acc = acc + jnp.dot(p_new.astype(BF16), vn_ref[...].astype(BF16), preferred_element_type=F32)
    o_ref[...] = (acc / denom).astype(o_ref.dtype)


def fox_sample(zq, k_cache, v_cache, k_new, v_new, c, *, batch, t_new, heads):
    past = k_cache.shape[1]
    c_len = c.shape[3]
    kern = functools.partial(_fox_sample_kernel, past=past, t_new=t_new)
    return pl.pallas_call(
        kern,
        grid=(batch, heads),
        in_specs=[
            pl.BlockSpec((t_new, HEAD_DIM), lambda b, h: (b, h)),
            pl.BlockSpec((1, past, HEAD_DIM), lambda b, h: (b, 0, h)),
            pl.BlockSpec((1, past, HEAD_DIM), lambda b, h: (b, 0, h)),
            pl.BlockSpec((t_new, HEAD_DIM), lambda b, h: (b, h)),
            pl.BlockSpec((t_new, HEAD_DIM), lambda b, h: (b, h)),
            pl.BlockSpec((1, 1, 1, c_len), lambda b, h: (b, h, 0, 0)),
        ],
        out_specs=pl.BlockSpec((t_new, HEAD_DIM), lambda b, h: (b, h)),
        out_shape=jax.ShapeDtypeStruct((batch * t_new, heads * HEAD_DIM), BF16),
        compiler_params=_params("parallel", "parallel"),
        name="fox_sample",
    )(zq, k_cache, v_cache, k_new, v_new, c)


def _top_rows(vals_list, count, payloads=None):
    n = len(vals_list)
    rows = [lax.broadcasted_iota(jnp.int32, v.shape, 0) for v in vals_list]
    vals = list(vals_list)
    out_v = [[] for _ in range(n)]
    out_i = [[] for _ in range(n)]
    for _ in range(count):
        for p in range(n):
            m = jnp.max(vals[p], axis=0, keepdims=True)
            idx = jnp.min(jnp.where(vals[p] == m, rows[p], vals[p].shape[0]), axis=0, keepdims=True)
            hit = rows[p] == idx
            out_v[p].append(m)
            out_i[p].append(idx if payloads is None
                            else jnp.max(jnp.where(hit, payloads[p], -1), axis=0, keepdims=True))
            vals[p] = jnp.where(hit, -jnp.inf, vals[p])
    return out_v, out_i


ROUTE_HEADS_PER_STEP = 4


def _peer_route_kernel(q_ref, keys_ref, eidx_ref, gate_ref, e_scr, g_scr):
    step = pl.program_id(1)
    q = q_ref[...].astype(BF16)
    hs = ROUTE_HEADS_PER_STEP
    scores = [lax.dot_general(keys_ref[hh, p].astype(BF16), q[:, (2 * hh + p) * LANES:(2 * hh + p + 1) * LANES], NT,
                              preferred_element_type=F32) for hh in range(hs) for p in range(2)]
    top_v, top_i = _top_rows(scores, PEER_TOPK)
    half = PEER_TOPK // 2
    sub = lax.broadcasted_iota(jnp.int32, (half, q.shape[0]), 0)
    cands, cidxs = [], []
    for hh in range(hs):
        hv0, hi0, hv1, hi1 = top_v[2 * hh], top_i[2 * hh], top_v[2 * hh + 1], top_i[2 * hh + 1]
        v0 = jnp.concatenate(hv0, axis=0)
        i0 = jnp.concatenate(hi0, axis=0)
        v1 = jnp.concatenate(hv1, axis=0)
        i1 = jnp.concatenate(hi1, axis=0)
        cand = [hv0[0] + v1]
        cidx = [hi0[0] * PEER_NKEYS + i1]
        for a in range(1, half):
            keep = sub < PEER_TOPK // (a + 1)
            cand.append(jnp.where(keep, hv0[a] + v1[:half], -jnp.inf))
            cidx.append(hi0[a] * PEER_NKEYS + i1[:half])
        cand.append(v0[half:] + hv1[0])
        cidx.append(i0[half:] * PEER_NKEYS + hi1[0])
        cands.append(jnp.concatenate(cand, axis=0))
        cidxs.append(jnp.concatenate(cidx, axis=0))
    bests, eidxs = _top_rows(cands, PEER_TOPK, payloads=cidxs)
    for hh in range(hs):
        best = jnp.concatenate(bests[hh], axis=0)
        e = jnp.exp(best - jnp.max(best, axis=0, keepdims=True))
        gate = e / jnp.sum(e, axis=0, keepdims=True)
        row0 = pl.multiple_of((step * hs + hh) * PEER_TOPK, PEER_TOPK)
        e_scr[pl.ds(row0, PEER_TOPK), :] = jnp.concatenate(eidxs[hh], axis=0)
        g_scr[pl.ds(row0, PEER_TOPK), :] = gate

    @pl.when(step == pl.num_programs(1) - 1)
    def _():
        eidx_ref[...] = e_scr[...].T
        gate_ref[...] = g_scr[...].T


def peer_route(q, sub_keys):
    n = q.shape[0]
    tt = min(256, n)
    hs = ROUTE_HEADS_PER_STEP
    assert n % tt == 0 and PEER_HEADS % hs == 0
    return pl.pallas_call(
        _peer_route_kernel,
        grid=(n // tt, PEER_HEADS // hs),
        in_specs=[
            pl.BlockSpec((tt, 2 * hs * LANES), lambda i, h: (i, h)),
            pl.BlockSpec((hs, 2, PEER_NKEYS, LANES), lambda i, h: (h, 0, 0, 0)),
        ],
        out_specs=[
            pl.BlockSpec((tt, PEER_PAIRS), lambda i, h: (i, 0)),
            pl.BlockSpec((tt, PEER_PAIRS), lambda i, h: (i, 0)),
        ],
        out_shape=[jax.ShapeDtypeStruct((n, PEER_PAIRS), jnp.int32), jax.ShapeDtypeStruct((n, PEER_PAIRS), F32)],
        scratch_shapes=[pltpu.VMEM((PEER_PAIRS, tt), jnp.int32), pltpu.VMEM((PEER_PAIRS, tt), F32)],
        compiler_params=_params("parallel", "arbitrary"),
        name="peer_route",
    )(q, sub_keys)


PEER_SPLIT = 8
PEER_CHUNK = PEER_SPLIT * LANES
PEER_ROWS = PEER_PAIRS * PEER_SPLIT
PACKED_ROWS = PEER_SPLIT // 2


def _peer_pass_kernel(*refs, mode, group, chunk):
    if mode == "u_first":
        idx_ref, h_ref, slab_hbm, o_ref, slab, stage, tiles, sem = refs
    elif mode == "u_last":
        idx_ref, h_ref, w_ref, gate_ref, slab_hbm, o_ref, slab, stage, tiles, sem = refs
    else:
        idx_ref, coef_ref, x_ref, slab_hbm, o_ref, slab, stage, tiles, sem = refs

    @pl.when(pl.program_id(0) == 0)
    def _():
        load = pltpu.make_async_copy(slab_hbm.at[chunk], slab, sem.at[0])
        load.start()
        load.wait()

    if mode != "v":
        for r in range(PEER_SPLIT):
            tiles[pl.ds(r, group, stride=PEER_SPLIT), :] = h_ref[:, r * LANES:(r + 1) * LANES]

    lane = lax.broadcasted_iota(jnp.int32, (PEER_SPLIT, PEER_ROWS), 1)
    diag = (lane % PEER_SPLIT) == lax.broadcasted_iota(jnp.int32, (PEER_SPLIT, PEER_ROWS), 0)

    def one_token(t, j):
        for k in range(PEER_PAIRS):
            row = pl.multiple_of(idx_ref[t, k], PACKED_ROWS)
            stage[j, pl.ds(k * PACKED_ROWS, PACKED_ROWS), :] = slab[pl.ds(row, PACKED_ROWS), :]
        m16 = pltpu.bitcast(stage[j], BF16)
        r0 = pl.multiple_of(t * PEER_SPLIT, PEER_SPLIT)
        if mode == "v":
            coef = coef_ref[pl.ds(t, 1), :]
            lhs = jnp.where(diag, jnp.broadcast_to(coef, (PEER_SPLIT, PEER_ROWS)), 0.0).astype(BF16)
            tiles[pl.ds(r0, PEER_SPLIT), :] = jnp.dot(lhs, m16, preferred_element_type=F32)
            return
        h16 = tiles[pl.ds(r0, PEER_SPLIT), :].astype(BF16)
        yt = lax.dot_general(h16, m16, NT, preferred_element_type=F32)
        o_ref[pl.ds(t, 1), :] = jnp.sum(jnp.where(diag, yt, 0.0), axis=0, keepdims=True)

    def body(i, carry):
        for j in range(PEER_UNROLL):
            one_token(i * PEER_UNROLL + j, j)
        return carry

    lax.fori_loop(0, group // PEER_UNROLL, body, 0)
    if mode == "v":
        lo, hi = chunk * PEER_CHUNK, (chunk + 1) * PEER_CHUNK
        if lo > 0:
            o_ref[:, :lo] = x_ref[:, :lo]
        if hi < x_ref.shape[1]:
            o_ref[:, hi:] = x_ref[:, hi:]
        for r in range(PEER_SPLIT):
            cols = slice(lo + r * LANES, lo + (r + 1) * LANES)
            o_ref[:, cols] = x_ref[:, cols] + tiles[pl.ds(r, group, stride=PEER_SPLIT), :]
    if mode == "u_last":
        w = o_ref[...] + w_ref[...]
        lane_g = lax.broadcasted_iota(jnp.int32, w.shape, 1)
        for s in (1, 2, 4):
            up = pltpu.roll(w, s, 1)
            down = pltpu.roll(w, PEER_ROWS - s, 1)
            w = w + jnp.where((lane_g & s) != 0, up, down)
        o_ref[...] = gate_ref[...] * jax.nn.gelu(w)


def peer_pass(mode, idx, slabs, *operands, group=64):
    n = idx.shape[0]
    group = min(group, n)
    assert n % group == 0 and group % PEER_UNROLL == 0
    vec = pl.BlockSpec((group, PEER_ROWS), lambda i: (i, 0))
    idx_spec = pl.BlockSpec((group, PEER_PAIRS), lambda i: (i, 0), memory_space=pltpu.SMEM)
    any_spec = pl.BlockSpec(memory_space=pl.ANY)
    chunk = operands[-1]
    cols = pl.BlockSpec((group, PEER_CHUNK), lambda i: (i, chunk))
    if mode == "v":
        coef, x = operands[:2]
        full = pl.BlockSpec((group, x.shape[1]), lambda i: (i, 0))
        in_specs, args = [idx_spec, vec, full, any_spec], (idx, coef, x, slabs)
        out_spec, out_shape = full, jax.ShapeDtypeStruct(x.shape, F32)
    else:
        h = operands[0]
        extra = list(operands[1:-1])
        in_specs, args = [idx_spec, cols] + [vec] * len(extra) + [any_spec], (idx, h, *extra, slabs)
        out_spec, out_shape = vec, jax.ShapeDtypeStruct((n, PEER_ROWS), F32)
    return pl.pallas_call(
        functools.partial(_peer_pass_kernel, mode=mode, group=group, chunk=chunk),
        grid=(n // group,),
        in_specs=in_specs,
        out_specs=out_spec,
        out_shape=out_shape,
        scratch_shapes=[
            pltpu.VMEM(slabs.shape[1:], jnp.uint32),
            pltpu.VMEM((PEER_UNROLL, PEER_PAIRS * PACKED_ROWS, LANES), jnp.uint32),
            pltpu.VMEM((group * PEER_SPLIT, LANES), F32),
            pltpu.SemaphoreType.DMA((1,)),
        ],
        compiler_params=_params("arbitrary"),
        name="peer_" + mode,
    )(*args)


def _pack_table_kernel(t_ref, o_ref, *, rows):
    x = t_ref[0]
    for c in range(o_ref.shape[0]):
        for j in range(PACKED_ROWS):
            col = c * PEER_CHUNK + 2 * j * LANES
            lo = pltpu.bitcast(x[:, col:col + LANES].astype(BF16).astype(F32), jnp.uint32)
            hi = pltpu.bitcast(x[:, col + LANES:col + 2 * LANES].astype(BF16).astype(F32), jnp.uint32)
            o_ref[c, pl.ds(j, rows, stride=PACKED_ROWS), :] = (lo >> 16) | (hi & jnp.uint32(0xFFFF0000))


def pack_table(tables, layer):
    _, e, d = tables.shape
    rows = min(512, e)
    assert e % rows == 0 and d % PEER_CHUNK == 0
    n_chunks = d // PEER_CHUNK
    return pl.pallas_call(
        functools.partial(_pack_table_kernel, rows=rows),
        grid=(e // rows,),
        in_specs=[pl.BlockSpec((1, rows, d), lambda i: (layer, i, 0))],
        out_specs=pl.BlockSpec((n_chunks, rows * PACKED_ROWS, LANES), lambda i: (0, i, 0)),
        out_shape=jax.ShapeDtypeStruct((n_chunks, e * PACKED_ROWS, LANES), jnp.uint32),
        compiler_params=_params("parallel"),
        name="pack_table",
    )(tables)


def peer_apply(x, h, eidx, gate, tables):
    u_slabs, v_slabs = tables
    n_chunks = u_slabs.shape[0]
    assert n_chunks == 2 and v_slabs.shape[0] == n_chunks
    idx = eidx * PACKED_ROWS
    gate8 = jnp.repeat(gate, PEER_SPLIT, axis=1)
    w = peer_pass("u_first", idx, u_slabs, h, 0)
    w = peer_pass("u_last", idx, u_slabs, h, w, gate8, 1)
    y = x
    for c in range(n_chunks):
        y = peer_pass("v", idx, v_slabs, w, y, c)
    return y


def _rms_kernel(x_ref, g_ref, o_ref):
    o_ref[...] = _rms(x_ref[...], g_ref[...])


def rms_norm(x, g):
    n, d = x.shape
    tm = min(512, n)
    return pl.pallas_call(
        _rms_kernel,
        grid=(n // tm,),
        in_specs=[pl.BlockSpec((tm, d), lambda i: (i, 0)), pl.BlockSpec((1, d), lambda i: (0, 0))],
        out_specs=pl.BlockSpec((tm, d), lambda i: (i, 0)),
        out_shape=jax.ShapeDtypeStruct((n, d), F32),
        compiler_params=_params("parallel"),
        name="rms_norm",
    )(x, g.reshape(1, d))


def peer_ffn(x, g, w_pq16, sub_keys, tables, g_final=None):
    q, h = norm_matmul(x, g, w_pq16, emit_norm=True)
    eidx, gate = peer_route(q, sub_keys)
    y = peer_apply(x, h, eidx, gate, tables)
    return y if g_final is None else rms_norm(y, g_final)


def _forget_inputs(zf, batch, t_new, t_pad):
    raw = zf[:, :FOX_HEADS_PAD].reshape(batch, t_new, FOX_HEADS_PAD)
    raw = jnp.swapaxes(raw, 1, 2)
    return jnp.pad(raw, ((0, 0), (0, 0), (0, t_pad - t_new)))


def kernel(x_prompt, x_sample, mem_prompt, cache_mem_k, cache_mem_v, cache_fox_k, cache_fox_v, cache_fox_logf, g_mix, g_ffn, w_o, g_mem, w_mem_kv, w_pq, peer_sub_keys, peer_u, peer_v, w_in_a, gmlp_ws, gmlp_b, gmlp_gv, w_in_b, g_kv, w_kv, b_f, g_final):
    batch, seq, d = x_prompt.shape
    dec_batch, dec_seq, _ = x_sample.shape
    depth = g_mix.shape[0]
    n_a = w_in_a.shape[0]
    n_mem = mem_prompt.shape[1]
    past = cache_fox_k.shape[1]
    heads = cache_fox_k.shape[2]
    fox_width = heads * HEAD_DIM
    gmlp_width = gmlp_gv.shape[1]
    assert gmlp_width == fox_width and d == fox_width + MEM_WIDTH

    w_o16 = w_o.astype(BF16)
    w_pq16 = w_pq.astype(BF16)
    w_in_a16 = w_in_a.astype(BF16)
    w_in_b16 = w_in_b.astype(BF16)
    w_mem16 = w_mem_kv.astype(BF16)
    w_k16 = w_kv[:, :fox_width].astype(BF16)
    w_v16 = w_kv[:, fox_width:2 * fox_width].astype(BF16)
    w_f16 = jnp.pad(w_kv[:, 2 * fox_width:], ((0, 0), (0, LANES - heads))).astype(BF16)
    b_col = jnp.pad(b_f, (0, FOX_HEADS_PAD - heads)).reshape(FOX_HEADS_PAD, 1)
    uv = [(pack_table(peer_u, l), pack_table(peer_v, l)) for l in range(depth)]

    def ffn(x, l, final=False):
        return peer_ffn(x, g_ffn[l], w_pq16[l], peer_sub_keys[l], uv[l], g_final if final else None)

    def shared_kv(x):
        k, k16 = norm_matmul(x, g_kv, w_k16, emit_bf16=True, split_heads=True)
        v, v16 = norm_matmul(x, g_kv, w_v16, emit_bf16=True, split_heads=True)
        zf = norm_matmul(x, g_kv, w_f16)
        return k, v, zf, k16, v16

    def run_group(x3, mem_k, mem_v, chunk, fox_fn):
        b, t, _ = x3.shape
        x = x3.reshape(b * t, d)
        gmlp_v = []
        extras = None
        for l in range(depth):
            if l < n_a:
                z = norm_matmul(x, g_mix[l], w_in_a16[l])
                part, v_rows = gmlp_mix(z, gmlp_ws[l], gmlp_b[l], gmlp_gv[l], chunk=chunk, width=gmlp_width)
                gmlp_v.append(v_rows.reshape(b, t, gmlp_width))
                x = mix_out(part, z, 2 * gmlp_width // MEM_WIDTH, mem_k[l], mem_v[l], w_o16[l], x, seq=t)
                x = ffn(x, l)
                if l == n_a - 1:
                    kvf = shared_kv(x)
            else:
                z = norm_matmul(x, g_mix[l], w_in_b16[l - n_a], out_dtype=BF16)
                part, extras = fox_fn(z, kvf, extras)
                x = mix_out(part, z, fox_width // MEM_WIDTH, mem_k[l], mem_v[l], w_o16[l], x, seq=t)
                x = ffn(x, l, final=(l == depth - 1))
        return x.reshape(b, t, d), gmlp_v, extras

    mem2d = mem_prompt.reshape(batch * n_mem, d)
    mem_k_p, mem_v_p = [], []
    for l in range(depth):
        zkv = norm_matmul(mem2d, g_mem[l], w_mem16[l])
        mem_k_p.append(zkv[:, :MEM_WIDTH].reshape(batch, n_mem, MEM_WIDTH))
        mem_v_p.append(zkv[:, MEM_WIDTH:].reshape(batch, n_mem, MEM_WIDTH))

    def fox_p(z, kvf, extras):
        if extras is None:
            k, v, zf, k16, v16 = kvf
            c, lf = logf_cumsum(_forget_inputs(zf, batch, seq, seq), b_col, seq)
            extras = (k, v, lf, c[:, :heads].reshape(batch, heads, 1, seq), k16, v16)
        k, v, lf, c, k16, v16 = extras
        return fox_prompt(z, k16, v16, c, batch=batch, seq=seq, heads=heads), extras

    y_p, _, ex_p = run_group(x_prompt, [a.astype(BF16) for a in mem_k_p], [a.astype(BF16) for a in mem_v_p],
                             min(seq, GMLP_CHUNK), fox_p)
    fox_k_p = ex_p[0].reshape(batch, seq, heads, HEAD_DIM)
    fox_v_p = ex_p[1].reshape(batch, seq, heads, HEAD_DIM)
    fox_logf_p = jnp.swapaxes(ex_p[2][:, :heads, :seq], 1, 2)
    new_mem_k = jnp.stack(mem_k_p).reshape(depth, batch, n_mem, MEM_HEADS, HEAD_DIM)
    new_mem_v = jnp.stack(mem_v_p).reshape(depth, batch, n_mem, MEM_HEADS, HEAD_DIM)

    t_pad = -(-dec_seq // LANES) * LANES
    prev_t = jnp.pad(jnp.swapaxes(cache_fox_logf, 1, 2), ((0, 0), (0, FOX_HEADS_PAD - heads), (0, 0)))
    k_cache = cache_fox_k.reshape(dec_batch, past, fox_width)
    v_cache = cache_fox_v.reshape(dec_batch, past, fox_width)

    def fox_s(z, kvf, extras):
        if extras is None:
            k, v, zf, k16, v16 = kvf
            c, lf = logf_cumsum(_forget_inputs(zf, dec_batch, dec_seq, t_pad), b_col, dec_seq, prev_t)
            extras = (k, v, lf, c[:, :heads].reshape(dec_batch, heads, 1, past + t_pad), k16, v16)
        k, v, lf, c, k16, v16 = extras
        return fox_sample(z, k_cache, v_cache, k16, v16, c, batch=dec_batch, t_new=dec_seq, heads=heads), extras

    mem_k_s = [cache_mem_k[l].reshape(dec_batch, n_mem, MEM_WIDTH).astype(BF16) for l in range(depth)]
    mem_v_s = [cache_mem_v[l].reshape(dec_batch, n_mem, MEM_WIDTH).astype(BF16) for l in range(depth)]
    y_s, gmlp_v_s, ex_s = run_group(x_sample, mem_k_s, mem_v_s, min(dec_seq, GMLP_CHUNK), fox_s)
    fox_k_s = ex_s[0].reshape(dec_batch, dec_seq, heads, HEAD_DIM)
    fox_v_s = ex_s[1].reshape(dec_batch, dec_seq, heads, HEAD_DIM)
    fox_logf_s = jnp.swapaxes(ex_s[2][:, :heads, :dec_seq], 1, 2)

    return (y_p, y_s, new_mem_k, new_mem_v, fox_k_p, fox_v_p, fox_logf_p,
            fox_k_s, fox_v_s, fox_logf_s, jnp.stack(gmlp_v_s))
```

```python
import functools

import jax
import jax.numpy as jnp
from jax import lax
from jax.experimental import pallas as pl
from jax.experimental.pallas import tpu as pltpu

EPS = 1e-6
NEG_INF = -1e30
LANES = 128
HEAD_DIM = 128
MEM_HEADS = 4
MEM_WIDTH = MEM_HEADS * HEAD_DIM
GMLP_GROUPS = 4
GMLP_CHUNK = 128
PEER_HEADS = 8
PEER_TOPK = 16
PEER_NKEYS = 128
PEER_PAIRS = PEER_HEADS * PEER_TOPK
PEER_UNROLL = 32
FOX_HEADS_PAD = 16
VMEM_LIMIT = 56 * 1024 * 1024
BF16 = jnp.bfloat16
F32 = jnp.float32
NT = (((1,), (1,)), ((), ()))


def _params(*sem):
    return pltpu.CompilerParams(dimension_semantics=sem, vmem_limit_bytes=VMEM_LIMIT)


def _rms(x, g):
    return x * lax.rsqrt(jnp.mean(x * x, axis=-1, keepdims=True) + EPS) * g


NORM_MATMUL_WEIGHT_BYTES = 16 * 1024 * 1024


def _norm_matmul_kernel(x_ref, g_ref, w_ref, o_ref, *rest, emit_norm, emit_bf16, split_heads):
    xn_ref = rest[-1]

    @pl.when(pl.program_id(1) == 0)
    def _():
        h = _rms(x_ref[...], g_ref[...])
        xn_ref[...] = h.astype(BF16)
        if emit_norm:
            rest[0][...] = h

    y = jnp.dot(xn_ref[...], w_ref[...], preferred_element_type=F32)
    if split_heads:
        for h in range(o_ref.shape[1]):
            o_ref[:, h, :] = y[:, h * HEAD_DIM:(h + 1) * HEAD_DIM].astype(o_ref.dtype)
    else:
        o_ref[...] = y.astype(o_ref.dtype)
    if emit_bf16:
        rest[-2][...] = y.astype(BF16)


def norm_matmul(x, g, w, *, out_dtype=F32, emit_norm=False, emit_bf16=False, split_heads=False):
    n, d = x.shape
    dout = w.shape[1]
    tm = min(512, n)
    tn = dout if d * dout * 2 <= NORM_MATMUL_WEIGHT_BYTES else 512
    assert n % tm == 0 and dout % tn == 0
    if split_heads:
        assert tn == dout and dout % HEAD_DIM == 0
        out_specs = [pl.BlockSpec((tm, dout // HEAD_DIM, HEAD_DIM), lambda i, j: (i, 0, 0))]
        out_shape = [jax.ShapeDtypeStruct((n, dout // HEAD_DIM, HEAD_DIM), out_dtype)]
    else:
        out_specs = [pl.BlockSpec((tm, tn), lambda i, j: (i, j))]
        out_shape = [jax.ShapeDtypeStruct((n, dout), out_dtype)]
    if emit_norm:
        out_specs.append(pl.BlockSpec((tm, d), lambda i, j: (i, 0)))
        out_shape.append(jax.ShapeDtypeStruct((n, d), F32))
    if emit_bf16:
        out_specs.append(pl.BlockSpec((tm, tn), lambda i, j: (i, j)))
        out_shape.append(jax.ShapeDtypeStruct((n, dout), BF16))
    out = pl.pallas_call(
        functools.partial(_norm_matmul_kernel, emit_norm=emit_norm, emit_bf16=emit_bf16, split_heads=split_heads),
        grid=(n // tm, dout // tn),
        in_specs=[
            pl.BlockSpec((tm, d), lambda i, j: (i, 0)),
            pl.BlockSpec((1, d), lambda i, j: (0, 0)),
            pl.BlockSpec((d, tn), lambda i, j: (0, j)),
        ],
        out_specs=out_specs,
        out_shape=out_shape,
        scratch_shapes=[pltpu.VMEM((tm, d), BF16)],
        compiler_params=_params("parallel", "arbitrary"),
        name="norm_matmul",
    )(x, g.reshape(1, d), w)
    return out[0] if len(out) == 1 else tuple(out)


def _gmlp_kernel(zu_ref, zv_ref, w_ref, b_ref, gv_ref, mix_ref, v_ref, *, chunk, n_chunks, gdim):
    tril = (lax.broadcasted_iota(jnp.int32, (chunk, chunk), 1)
            <= lax.broadcasted_iota(jnp.int32, (chunk, chunk), 0))
    for c in range(n_chunks):
        rows = slice(c * chunk, (c + 1) * chunk)
        u = jax.nn.gelu(zu_ref[rows, :])
        v = _rms(jax.nn.gelu(zv_ref[rows, :]), gv_ref[...])
        v_ref[rows, :] = v
        v16 = v.astype(BF16)
        for g in range(GMLP_GROUPS):
            cols = slice(g * gdim, (g + 1) * gdim)
            w = jnp.where(tril, w_ref[g], 0.0).astype(BF16)
            mixed = jnp.dot(w, v16[:, cols], preferred_element_type=F32) + b_ref[g]
            mix_ref[rows, cols] = (u[:, cols] * mixed).astype(BF16)


def gmlp_mix(z, w_s, b_s, g_v, *, chunk, width):
    n = z.shape[0]
    tm = min(512, n)
    assert n % tm == 0 and tm % chunk == 0
    gdim = width // GMLP_GROUPS
    w = w_s[:, :chunk, :chunk]
    b = b_s[:, :chunk].reshape(GMLP_GROUPS, chunk, 1)
    kern = functools.partial(_gmlp_kernel, chunk=chunk, n_chunks=tm // chunk, gdim=gdim)
    return pl.pallas_call(
        kern,
        grid=(n // tm,),
        in_specs=[
            pl.BlockSpec((tm, width), lambda i: (i, 0)),
            pl.BlockSpec((tm, width), lambda i: (i, 1)),
            pl.BlockSpec((GMLP_GROUPS, chunk, chunk), lambda i: (0, 0, 0)),
            pl.BlockSpec((GMLP_GROUPS, chunk, 1), lambda i: (0, 0, 0)),
            pl.BlockSpec((1, width), lambda i: (0, 0)),
        ],
        out_specs=[
            pl.BlockSpec((tm, width), lambda i: (i, 0)),
            pl.BlockSpec((tm, width), lambda i: (i, 0)),
        ],
        out_shape=[jax.ShapeDtypeStruct((n, width), BF16), jax.ShapeDtypeStruct((n, width), F32)],
        compiler_params=_params("parallel"),
        name="gmlp_mix",
    )(z, z, w, b, g_v.reshape(1, width))


def _mix_out_kernel(part_ref, qm_ref, mk_ref, mv_ref, wo_ref, x_ref, o_ref, *, n_sub, t_sub, width):
    scale = HEAD_DIM ** -0.5
    subs = []
    for bi in range(n_sub):
        rows = slice(bi * t_sub, (bi + 1) * t_sub)
        heads = []
        for h in range(MEM_HEADS):
            cols = slice(h * HEAD_DIM, (h + 1) * HEAD_DIM)
            q = qm_ref[rows, cols].astype(BF16)
            s = lax.dot_general(q, mk_ref[bi, :, cols], NT, preferred_element_type=F32) * scale
            s = s - jnp.max(s, axis=-1, keepdims=True)
            p = jnp.exp(s)
            p = (p / jnp.sum(p, axis=-1, keepdims=True)).astype(BF16)
            heads.append(jnp.dot(p, mv_ref[bi, :, cols], preferred_element_type=F32))
        subs.append(jnp.concatenate(heads, axis=1))
    mem = (subs[0] if n_sub == 1 else jnp.concatenate(subs, axis=0)).astype(BF16)
    y = jnp.dot(part_ref[...], wo_ref[:width, :], preferred_element_type=F32)
    y = y + jnp.dot(mem, wo_ref[width:, :], preferred_element_type=F32)
    o_ref[...] = x_ref[...] + y


def mix_out(part, z, q_block, mk, mv, w_o, x, *, seq):
    n, d = x.shape
    width = part.shape[1]
    n_mem = mk.shape[1]
    tm = min(256, n)
    assert n % tm == 0
    if seq >= tm:
        assert seq % tm == 0
        n_sub, t_sub = 1, tm
        kv_map = lambda i: ((i * tm) // seq, 0, 0)
    else:
        assert tm % seq == 0
        n_sub, t_sub = tm // seq, seq
        kv_map = lambda i: (i, 0, 0)
    kern = functools.partial(_mix_out_kernel, n_sub=n_sub, t_sub=t_sub, width=width)
    return pl.pallas_call(
        kern,
        grid=(n // tm,),
        in_specs=[
            pl.BlockSpec((tm, width), lambda i: (i, 0)),
            pl.BlockSpec((tm, MEM_WIDTH), lambda i: (i, q_block)),
            pl.BlockSpec((n_sub, n_mem, MEM_WIDTH), kv_map),
            pl.BlockSpec((n_sub, n_mem, MEM_WIDTH), kv_map),
            pl.BlockSpec((width + MEM_WIDTH, d), lambda i: (0, 0)),
            pl.BlockSpec((tm, d), lambda i: (i, 0)),
        ],
        out_specs=pl.BlockSpec((tm, d), lambda i: (i, 0)),
        out_shape=jax.ShapeDtypeStruct((n, d), F32),
        compiler_params=_params("parallel"),
        name="mix_out",
    )(part, z, mk, mv, w_o, x)


def _log_sigmoid(x):
    return jnp.minimum(x, 0.0) - jnp.log1p(jnp.exp(-jnp.abs(x)))


def _cumsum_kernel(*refs, has_prev, t_new):
    if has_prev:
        prev_ref, raw_ref, b_ref, c_ref, lf_ref = refs
    else:
        raw_ref, b_ref, c_ref, lf_ref = refs
    raw = raw_ref[0]
    lf = _log_sigmoid(raw + b_ref[...])
    lf = jnp.where(lax.broadcasted_iota(jnp.int32, lf.shape, 1) < t_new, lf, 0.0)
    lf_ref[0] = lf
    full = jnp.concatenate([prev_ref[0], lf], axis=1) if has_prev else lf
    total = full.shape[1]
    lane = lax.broadcasted_iota(jnp.int32, full.shape, 1)
    shift = 1
    while shift < total:
        full = full + jnp.where(lane >= shift, pltpu.roll(full, shift, 1), 0.0)
        shift *= 2
    c_ref[0] = full


def logf_cumsum(raw_t, b_col, t_new, prev_t=None):
    nb, hp, t_pad = raw_t.shape
    past = 0 if prev_t is None else prev_t.shape[2]
    kern = functools.partial(_cumsum_kernel, has_prev=prev_t is not None, t_new=t_new)
    in_specs = [pl.BlockSpec((1, hp, t_pad), lambda i: (i, 0, 0)), pl.BlockSpec((hp, 1), lambda i: (0, 0))]
    args = [raw_t, b_col]
    if prev_t is not None:
        in_specs = [pl.BlockSpec((1, hp, past), lambda i: (i, 0, 0))] + in_specs
        args = [prev_t] + args
    return pl.pallas_call(
        kern,
        grid=(nb,),
        in_specs=in_specs,
        out_specs=[
            pl.BlockSpec((1, hp, past + t_pad), lambda i: (i, 0, 0)),
            pl.BlockSpec((1, hp, t_pad), lambda i: (i, 0, 0)),
        ],
        out_shape=[jax.ShapeDtypeStruct((nb, hp, past + t_pad), F32), jax.ShapeDtypeStruct((nb, hp, t_pad), F32)],
        compiler_params=_params("parallel"),
        name="logf_cumsum",
    )(*args)


FOX_TILE = 2048
FOX_HEADS_PER_STEP = 2


def _fox_kernel(qi_ref, ki_ref, q_ref, k_ref, v_ref, ck_ref, cq_ref, o_ref, m_ref, l_ref, acc_ref, *, tile):
    step = pl.program_id(2)
    qi = qi_ref[step]
    ki = ki_ref[step]

    @pl.when(ki == 0)
    def _():
        m_ref[...] = jnp.full(m_ref.shape, NEG_INF, F32)
        l_ref[...] = jnp.zeros(l_ref.shape, F32)
        acc_ref[...] = jnp.zeros(acc_ref.shape, F32)

    def block(masked):
        for hh in range(FOX_HEADS_PER_STEP):
            cols = slice(hh * HEAD_DIM, (hh + 1) * HEAD_DIM)
            s = lax.dot_general(q_ref[:, cols], k_ref[:, cols], NT, preferred_element_type=F32) * (HEAD_DIM ** -0.5)
            s = s - (ck_ref[0, hh] - cq_ref[0, hh, :, 0:1])
            if masked:
                qpos = lax.broadcasted_iota(jnp.int32, (tile, tile), 0)
                kpos = lax.broadcasted_iota(jnp.int32, (tile, tile), 1)
                s = jnp.where(kpos <= qpos, s, NEG_INF)
            m_prev = m_ref[hh]
            m_new = jnp.maximum(m_prev, jnp.max(s, axis=-1, keepdims=True))
            alpha = jnp.exp(m_prev - m_new)
            p = jnp.exp(s - m_new)
            l_ref[hh] = alpha * l_ref[hh] + jnp.sum(p, axis=-1, keepdims=True)
            acc_ref[:, cols] = alpha * acc_ref[:, cols] + jnp.dot(p.astype(BF16), v_ref[:, cols],
                                                                  preferred_element_type=F32)
            m_ref[hh] = m_new

    @pl.when(ki < qi)
    def _():
        block(False)

    @pl.when(ki == qi)
    def _():
        block(True)
        for hh in range(FOX_HEADS_PER_STEP):
            cols = slice(hh * HEAD_DIM, (hh + 1) * HEAD_DIM)
            o_ref[:, cols] = (acc_ref[:, cols] / l_ref[hh]).astype(o_ref.dtype)


def fox_prompt(zq, k, v, c, *, batch, seq, heads):
    n = zq.shape[0]
    tile = min(FOX_TILE, seq)
    nt = seq // tile
    hs = FOX_HEADS_PER_STEP
    assert heads % hs == 0
    width = hs * HEAD_DIM
    pairs = [(qi, ki) for qi in range(nt) for ki in range(qi + 1)]
    qi_tab = jnp.asarray([p[0] for p in pairs], jnp.int32)
    ki_tab = jnp.asarray([p[1] for p in pairs], jnp.int32)
    q_map = lambda b, h, s, qt, kt: (b * nt + qt[s], h)
    kv_map = lambda b, h, s, qt, kt: (b * nt + kt[s], h)
    grid_spec = pltpu.PrefetchScalarGridSpec(
        num_scalar_prefetch=2,
        grid=(batch, heads // hs, len(pairs)),
        in_specs=[
            pl.BlockSpec((tile, width), q_map),
            pl.BlockSpec((tile, width), kv_map),
            pl.BlockSpec((tile, width), kv_map),
            pl.BlockSpec((1, hs, 1, tile), lambda b, h, s, qt, kt: (b, h, 0, kt[s])),
            pl.BlockSpec((1, hs, 1, tile), lambda b, h, s, qt, kt: (b, h, 0, qt[s])),
        ],
        out_specs=pl.BlockSpec((tile, width), q_map),
        scratch_shapes=[pltpu.VMEM((hs, tile, 1), F32), pltpu.VMEM((hs, tile, 1), F32),
                        pltpu.VMEM((tile, width), F32)],
    )
    return pl.pallas_call(
        functools.partial(_fox_kernel, tile=tile),
        grid_spec=grid_spec,
        out_shape=jax.ShapeDtypeStruct((n, heads * HEAD_DIM), BF16),
        compiler_params=_params("parallel", "parallel", "arbitrary"),
        name="fox_prompt",
    )(qi_tab, ki_tab, zq, k, v, c, c)


def _fox_sample_kernel(q_ref, kc_ref, vc_ref, kn_ref, vn_ref, c_ref, o_ref, *, past, t_new):
    scale = HEAD_DIM ** -0.5
    q = q_ref[...].astype(BF16)
    c = c_ref[0, 0]
    c_ref0 = c[:, past:past + 1]
    s_old = lax.dot_general(q, kc_ref[0].astype(BF16), NT, preferred_element_type=F32) * scale
    s_old = s_old - (c[:, :past] - c_ref0)
    s_new = lax.dot_general(q, kn_ref[...].astype(BF16), NT, preferred_element_type=F32) * scale
    s_new = s_new - (c[:, past:past + t_new] - c_ref0)
    causal = (lax.broadcasted_iota(jnp.int32, (t_new, t_new), 1)
              <= lax.broadcasted_iota(jnp.int32, (t_new, t_new), 0))
    s_new = jnp.where(causal, s_new, NEG_INF)
    m = jnp.maximum(jnp.max(s_old, axis=-1, keepdims=True), jnp.max(s_new, axis=-1, keepdims=True))
    p_old = jnp.exp(s_old - m)
    p_new = jnp.exp(s_new - m)
    denom = jnp.sum(p_old, axis=-1, keepdims=True) + jnp.sum(p_new, axis=-1, keepdims=True)
    acc = jnp.dot(p_old.astype(BF16), vc_ref[0].astype(BF16), preferred_element_type=F32)
    acc = acc + jnp.dot(p_new.astype(BF16), vn_ref[...].astype(BF16), preferred_element_type=F32)
    o_ref[...] = (acc / denom).astype(o_ref.dtype)


def fox_sample(zq, k_cache, v_cache, k_new, v_new, c, *, batch, t_new, heads):
    past = k_cache.shape[1]
    c_len = c.shape[3]
    kern = functools.partial(_fox_sample_kernel, past=past, t_new=t_new)
    return pl.pallas_call(
        kern,
        grid=(batch, heads),
        in_specs=[
            pl.BlockSpec((t_new, HEAD_DIM), lambda b, h: (b, h)),
            pl.BlockSpec((1, past, HEAD_DIM), lambda b, h: (b, 0, h)),
            pl.BlockSpec((1, past, HEAD_DIM), lambda b, h: (b, 0, h)),
            pl.BlockSpec((t_new, HEAD_DIM), lambda b, h: (b, h)),
            pl.BlockSpec((t_new, HEAD_DIM), lambda b, h: (b, h)),
            pl.BlockSpec((1, 1, 1, c_len), lambda b, h: (b, h, 0, 0)),
        ],
        out_specs=pl.BlockSpec((t_new, HEAD_DIM), lambda b, h: (b, h)),
        out_shape=jax.ShapeDtypeStruct((batch * t_new, heads * HEAD_DIM), BF16),
        compiler_params=_params("parallel", "parallel"),
        name="fox_sample",
    )(zq, k_cache, v_cache, k_new, v_new, c)


def _top_rows(vals_list, count, payloads=None):
    n = len(vals_list)
    rows = [lax.broadcasted_iota(jnp.int32, v.shape, 0) for v in vals_list]
    vals = list(vals_list)
    out_v = [[] for _ in range(n)]
    out_i = [[] for _ in range(n)]
    for _ in range(count):
        for p in range(n):
            m = jnp.max(vals[p], axis=0, keepdims=True)
            idx = jnp.min(jnp.where(vals[p] == m, rows[p], vals[p].shape[0]), axis=0, keepdims=True)
            hit = rows[p] == idx
            out_v[p].append(m)
            out_i[p].append(idx if payloads is None
                            else jnp.max(jnp.where(hit, payloads[p], -1), axis=0, keepdims=True))
            vals[p] = jnp.where(hit, -jnp.inf, vals[p])
    return out_v, out_i


ROUTE_HEADS_PER_STEP = 4


def _peer_route_kernel(q_ref, keys_ref, eidx_ref, gate_ref, e_scr, g_scr):
    step = pl.program_id(1)
    q = q_ref[...].astype(BF16)
    hs = ROUTE_HEADS_PER_STEP
    scores = [lax.dot_general(keys_ref[hh, p].astype(BF16), q[:, (2 * hh + p) * LANES:(2 * hh + p + 1) * LANES], NT,
                              preferred_element_type=F32) for hh in range(hs) for p in range(2)]
    top_v, top_i = _top_rows(scores, PEER_TOPK)
    half = PEER_TOPK // 2
    sub = lax.broadcasted_iota(jnp.int32, (half, q.shape[0]), 0)
    cands, cidxs = [], []
    for hh in range(hs):
        hv0, hi0, hv1, hi1 = top_v[2 * hh], top_i[2 * hh], top_v[2 * hh + 1], top_i[2 * hh + 1]
        v0 = jnp.concatenate(hv0, axis=0)
        i0 = jnp.concatenate(hi0, axis=0)
        v1 = jnp.concatenate(hv1, axis=0)
        i1 = jnp.concatenate(hi1, axis=0)
        cand = [hv0[0] + v1]
        cidx = [hi0[0] * PEER_NKEYS + i1]
        for a in range(1, half):
            keep = sub < PEER_TOPK // (a + 1)
            cand.append(jnp.where(keep, hv0[a] + v1[:half], -jnp.inf))
            cidx.append(hi0[a] * PEER_NKEYS + i1[:half])
        cand.append(v0[half:] + hv1[0])
        cidx.append(i0[half:] * PEER_NKEYS + hi1[0])
        cands.append(jnp.concatenate(cand, axis=0))
        cidxs.append(jnp.concatenate(cidx, axis=0))
    bests, eidxs = _top_rows(cands, PEER_TOPK, payloads=cidxs)
    for hh in range(hs):
        best = jnp.concatenate(bests[hh], axis=0)
        e = jnp.exp(best - jnp.max(best, axis=0, keepdims=True))
        gate = e / jnp.sum(e, axis=0, keepdims=True)
        row0 = pl.multiple_of((step * hs + hh) * PEER_TOPK, PEER_TOPK)
        e_scr[pl.ds(row0, PEER_TOPK), :] = jnp.concatenate(eidxs[hh], axis=0)
        g_scr[pl.ds(row0, PEER_TOPK), :] = gate

    @pl.when(step == pl.num_programs(1) - 1)
    def _():
        eidx_ref[...] = e_scr[...].T
        g = g_scr[...]
        g8 = jnp.broadcast_to(g[:, None, :], (PEER_PAIRS, 8, g.shape[1])).reshape(PEER_PAIRS * 8, g.shape[1])
        gate_ref[...] = g8.T


def peer_route(q, sub_keys):
    n = q.shape[0]
    tt = min(256, n)
    hs = ROUTE_HEADS_PER_STEP
    assert n % tt == 0 and PEER_HEADS % hs == 0
    return pl.pallas_call(
        _peer_route_kernel,
        grid=(n // tt, PEER_HEADS // hs),
        in_specs=[
            pl.BlockSpec((tt, 2 * hs * LANES), lambda i, h: (i, h)),
            pl.BlockSpec((hs, 2, PEER_NKEYS, LANES), lambda i, h: (h, 0, 0, 0)),
        ],
        out_specs=[
            pl.BlockSpec((tt, PEER_PAIRS), lambda i, h: (i, 0)),
            pl.BlockSpec((tt, PEER_PAIRS * 8), lambda i, h: (i, 0)),
        ],
        out_shape=[jax.ShapeDtypeStruct((n, PEER_PAIRS), jnp.int32), jax.ShapeDtypeStruct((n, PEER_PAIRS * 8), F32)],
        scratch_shapes=[pltpu.VMEM((PEER_PAIRS, tt), jnp.int32), pltpu.VMEM((PEER_PAIRS, tt), F32)],
        compiler_params=_params("parallel", "arbitrary"),
        name="peer_route",
    )(q, sub_keys)


PEER_SPLIT = 8
PEER_CHUNK = PEER_SPLIT * LANES
PEER_ROWS = PEER_PAIRS * PEER_SPLIT
PACKED_ROWS = PEER_SPLIT // 2


def _peer_pass_kernel(*refs, mode, group, chunk):
    if mode == "u_first":
        idx_ref, h_ref, slab_hbm, o_ref, slab, stage, tiles, sem = refs
    elif mode == "u_last":
        idx_ref, h_ref, w_ref, gate_ref, slab_hbm, o_ref, slab, stage, tiles, sem = refs
    else:
        idx_ref, coef_ref, x_ref, slab_hbm, o_ref, slab, stage, tiles, sem = refs

    @pl.when(pl.program_id(0) == 0)
    def _():
        load = pltpu.make_async_copy(slab_hbm.at[chunk], slab, sem.at[0])
        load.start()
        load.wait()

    if mode != "v":
        for r in range(PEER_SPLIT):
            tiles[pl.ds(r, group, stride=PEER_SPLIT), :] = h_ref[:, r * LANES:(r + 1) * LANES]

    lane = lax.broadcasted_iota(jnp.int32, (PEER_SPLIT, PEER_ROWS), 1)
    diag = (lane % PEER_SPLIT) == lax.broadcasted_iota(jnp.int32, (PEER_SPLIT, PEER_ROWS), 0)

    def one_token(t, j):
        for k in range(PEER_PAIRS):
            row = pl.multiple_of(idx_ref[t, k], PACKED_ROWS)
            stage[j, pl.ds(k * PACKED_ROWS, PACKED_ROWS), :] = slab[pl.ds(row, PACKED_ROWS), :]
        m16 = pltpu.bitcast(stage[j], BF16)
        r0 = pl.multiple_of(t * PEER_SPLIT, PEER_SPLIT)
        if mode == "v":
            coef = coef_ref[pl.ds(t, 1), :]
            lhs = jnp.where(diag, jnp.broadcast_to(coef, (PEER_SPLIT, PEER_ROWS)), 0.0).astype(BF16)
            tiles[pl.ds(r0, PEER_SPLIT), :] = jnp.dot(lhs, m16, preferred_element_type=F32)
            return
        h16 = tiles[pl.ds(r0, PEER_SPLIT), :].astype(BF16)
        yt = lax.dot_general(h16, m16, NT, preferred_element_type=F32)
        o_ref[pl.ds(t, 1), :] = jnp.sum(jnp.where(diag, yt, 0.0), axis=0, keepdims=True)

    def body(i, carry):
        for j in range(PEER_UNROLL):
            one_token(i * PEER_UNROLL + j, j)
        return carry

    lax.fori_loop(0, group // PEER_UNROLL, body, 0)
    if mode == "v":
        lo, hi = chunk * PEER_CHUNK, (chunk + 1) * PEER_CHUNK
        if lo > 0:
            o_ref[:, :lo] = x_ref[:, :lo]
        if hi < x_ref.shape[1]:
            o_ref[:, hi:] = x_ref[:, hi:]
        for r in range(PEER_SPLIT):
            cols = slice(lo + r * LANES, lo + (r + 1) * LANES)
            o_ref[:, cols] = x_ref[:, cols] + tiles[pl.ds(r, group, stride=PEER_SPLIT), :]
    if mode == "u_last":
        w = o_ref[...] + w_ref[...]
        lane_g = lax.broadcasted_iota(jnp.int32, w.shape, 1)
        for s in (1, 2, 4):
            up = pltpu.roll(w, s, 1)
            down = pltpu.roll(w, PEER_ROWS - s, 1)
            w = w + jnp.where((lane_g & s) != 0, up, down)
        o_ref[...] = gate_ref[...] * jax.nn.gelu(w)


def peer_pass(mode, idx, slabs, *operands, group=64):
    n = idx.shape[0]
    group = min(group, n)
    assert n % group == 0 and group % PEER_UNROLL == 0
    vec = pl.BlockSpec((group, PEER_ROWS), lambda i: (i, 0))
    idx_spec = pl.BlockSpec((group, PEER_PAIRS), lambda i: (i, 0), memory_space=pltpu.SMEM)
    any_spec = pl.BlockSpec(memory_space=pl.ANY)
    chunk = operands[-1]
    cols = pl.BlockSpec((group, PEER_CHUNK), lambda i: (i, chunk))
    if mode == "v":
        coef, x = operands[:2]
        full = pl.BlockSpec((group, x.shape[1]), lambda i: (i, 0))
        in_specs, args = [idx_spec, vec, full, any_spec], (idx, coef, x, slabs)
        out_spec, out_shape = full, jax.ShapeDtypeStruct(x.shape, F32)
    else:
        h = operands[0]
        extra = list(operands[1:-1])
        in_specs, args = [idx_spec, cols] + [vec] * len(extra) + [any_spec], (idx, h, *extra, slabs)
        out_spec, out_shape = vec, jax.ShapeDtypeStruct((n, PEER_ROWS), F32)
    return pl.pallas_call(
        functools.partial(_peer_pass_kernel, mode=mode, group=group, chunk=chunk),
        grid=(n // group,),
        in_specs=in_specs,
        out_specs=out_spec,
        out_shape=out_shape,
        scratch_shapes=[
            pltpu.VMEM(slabs.shape[1:], jnp.uint32),
            pltpu.VMEM((PEER_UNROLL, PEER_PAIRS * PACKED_ROWS, LANES), jnp.uint32),
            pltpu.VMEM((group * PEER_SPLIT, LANES), F32),
            pltpu.SemaphoreType.DMA((1,)),
        ],
        compiler_params=_params("arbitrary"),
        name="peer_" + mode,
    )(*args)


def _pack_table_kernel(t_ref, o_ref, *, rows):
    x = t_ref[0]
    for c in range(o_ref.shape[0]):
        for j in range(PACKED_ROWS):
            col = c * PEER_CHUNK + 2 * j * LANES
            lo = pltpu.bitcast(x[:, col:col + LANES].astype(BF16).astype(F32), jnp.uint32)
            hi = pltpu.bitcast(x[:, col + LANES:col + 2 * LANES].astype(BF16).astype(F32), jnp.uint32)
            o_ref[c, pl.ds(j, rows, stride=PACKED_ROWS), :] = (lo >> 16) | (hi & jnp.uint32(0xFFFF0000))


def pack_table(tables, layer):
    _, e, d = tables.shape
    rows = min(512, e)
    assert e % rows == 0 and d % PEER_CHUNK == 0
    n_chunks = d // PEER_CHUNK
    return pl.pallas_call(
        functools.partial(_pack_table_kernel, rows=rows),
        grid=(e // rows,),
        in_specs=[pl.BlockSpec((1, rows, d), lambda i: (layer, i, 0))],
        out_specs=pl.BlockSpec((n_chunks, rows * PACKED_ROWS, LANES), lambda i: (0, i, 0)),
        out_shape=jax.ShapeDtypeStruct((n_chunks, e * PACKED_ROWS, LANES), jnp.uint32),
        compiler_params=_params("parallel"),
        name="pack_table",
    )(tables)


def peer_apply(x, h, eidx, gate8, tables):
    u_slabs, v_slabs = tables
    n_chunks = u_slabs.shape[0]
    assert n_chunks == 2 and v_slabs.shape[0] == n_chunks
    idx = eidx * PACKED_ROWS
    w = peer_pass("u_first", idx, u_slabs, h, 0)
    w = peer_pass("u_last", idx, u_slabs, h, w, gate8, 1)
    y = x
    for c in range(n_chunks):
        y = peer_pass("v", idx, v_slabs, w, y, c)
    return y


def _rms_kernel(x_ref, g_ref, o_ref):
    o_ref[...] = _rms(x_ref[...], g_ref[...])


def rms_norm(x, g):
    n, d = x.shape
    tm = min(512, n)
    return pl.pallas_call(
        _rms_kernel,
        grid=(n // tm,),
        in_specs=[pl.BlockSpec((tm, d), lambda i: (i, 0)), pl.BlockSpec((1, d), lambda i: (0, 0))],
        out_specs=pl.BlockSpec((tm, d), lambda i: (i, 0)),
        out_shape=jax.ShapeDtypeStruct((n, d), F32),
        compiler_params=_params("parallel"),
        name="rms_norm",
    )(x, g.reshape(1, d))


def peer_ffn(x, g, w_pq16, sub_keys, tables, g_final=None):
    q, h = norm_matmul(x, g, w_pq16, emit_norm=True)
    eidx, gate8 = peer_route(q, sub_keys)
    y = peer_apply(x, h, eidx, gate8, tables)
    return y if g_final is None else rms_norm(y, g_final)


def _forget_inputs(zf, batch, t_new, t_pad):
    raw = zf[:, :FOX_HEADS_PAD].reshape(batch, t_new, FOX_HEADS_PAD)
    raw = jnp.swapaxes(raw, 1, 2)
    return jnp.pad(raw, ((0, 0), (0, 0), (0, t_pad - t_new)))


def kernel(x_prompt, x_sample, mem_prompt, cache_mem_k, cache_mem_v, cache_fox_k, cache_fox_v, cache_fox_logf, g_mix, g_ffn, w_o, g_mem, w_mem_kv, w_pq, peer_sub_keys, peer_u, peer_v, w_in_a, gmlp_ws, gmlp_b, gmlp_gv, w_in_b, g_kv, w_kv, b_f, g_final):
    batch, seq, d = x_prompt.shape
    dec_batch, dec_seq, _ = x_sample.shape
    depth = g_mix.shape[0]
    n_a = w_in_a.shape[0]
    n_mem = mem_prompt.shape[1]
    past = cache_fox_k.shape[1]
    heads = cache_fox_k.shape[2]
    fox_width = heads * HEAD_DIM
    gmlp_width = gmlp_gv.shape[1]
    assert gmlp_width == fox_width and d == fox_width + MEM_WIDTH

    w_o16 = w_o.astype(BF16)
    w_pq16 = w_pq.astype(BF16)
    w_in_a16 = w_in_a.astype(BF16)
    w_in_b16 = w_in_b.astype(BF16)
    w_mem16 = w_mem_kv.astype(BF16)
    w_k16 = w_kv[:, :fox_width].astype(BF16)
    w_v16 = w_kv[:, fox_width:2 * fox_width].astype(BF16)
    w_f16 = jnp.pad(w_kv[:, 2 * fox_width:], ((0, 0), (0, LANES - heads))).astype(BF16)
    b_col = jnp.pad(b_f, (0, FOX_HEADS_PAD - heads)).reshape(FOX_HEADS_PAD, 1)
    uv = [(pack_table(peer_u, l), pack_table(peer_v, l)) for l in range(depth)]

    def ffn(x, l, final=False):
        return peer_ffn(x, g_ffn[l], w_pq16[l], peer_sub_keys[l], uv[l], g_final if final else None)

    def shared_kv(x):
        k, k16 = norm_matmul(x, g_kv, w_k16, emit_bf16=True, split_heads=True)
        v, v16 = norm_matmul(x, g_kv, w_v16, emit_bf16=True, split_heads=True)
        zf = norm_matmul(x, g_kv, w_f16)
        return k, v, zf, k16, v16

    def run_group(x3, mem_k, mem_v, chunk, fox_fn):
        b, t, _ = x3.shape
        x = x3.reshape(b * t, d)
        gmlp_v = []
        extras = None
        for l in range(depth):
            if l < n_a:
                z = norm_matmul(x, g_mix[l], w_in_a16[l])
                part, v_rows = gmlp_mix(z, gmlp_ws[l], gmlp_b[l], gmlp_gv[l], chunk=chunk, width=gmlp_width)
                gmlp_v.append(v_rows.reshape(b, t, gmlp_width))
                x = mix_out(part, z, 2 * gmlp_width // MEM_WIDTH, mem_k[l], mem_v[l], w_o16[l], x, seq=t)
                x = ffn(x, l)
                if l == n_a - 1:
                    kvf = shared_kv(x)
            else:
                z = norm_matmul(x, g_mix[l], w_in_b16[l - n_a], out_dtype=BF16)
                part, extras = fox_fn(z, kvf, extras)
                x = mix_out(part, z, fox_width // MEM_WIDTH, mem_k[l], mem_v[l], w_o16[l], x, seq=t)
                x = ffn(x, l, final=(l == depth - 1))
        return x.reshape(b, t, d), gmlp_v, extras

    mem2d = mem_prompt.reshape(batch * n_mem, d)
    mem_k_p, mem_v_p = [], []
    for l in range(depth):
        zkv = norm_matmul(mem2d, g_mem[l], w_mem16[l])
        mem_k_p.append(zkv[:, :MEM_WIDTH].reshape(batch, n_mem, MEM_WIDTH))
        mem_v_p.append(zkv[:, MEM_WIDTH:].reshape(batch, n_mem, MEM_WIDTH))

    def fox_p(z, kvf, extras):
        if extras is None:
            k, v, zf, k16, v16 = kvf
            c, lf = logf_cumsum(_forget_inputs(zf, batch, seq, seq), b_col, seq)
            extras = (k, v, lf, c[:, :heads].reshape(batch, heads, 1, seq), k16, v16)
        k, v, lf, c, k16, v16 = extras
        return fox_prompt(z, k16, v16, c, batch=batch, seq=seq, heads=heads), extras

    y_p, _, ex_p = run_group(x_prompt, [a.astype(BF16) for a in mem_k_p], [a.astype(BF16) for a in mem_v_p],
                             min(seq, GMLP_CHUNK), fox_p)
    fox_k_p = ex_p[0].reshape(batch, seq, heads, HEAD_DIM)
    fox_v_p = ex_p[1].reshape(batch, seq, heads, HEAD_DIM)
    fox_logf_p = jnp.swapaxes(ex_p[2][:, :heads, :seq], 1, 2)
    new_mem_k = jnp.stack(mem_k_p).reshape(depth, batch, n_mem, MEM_HEADS, HEAD_DIM)
    new_mem_v = jnp.stack(mem_v_p).reshape(depth, batch, n_mem, MEM_HEADS, HEAD_DIM)

    t_pad = -(-dec_seq // LANES) * LANES
    prev_t = jnp.pad(jnp.swapaxes(cache_fox_logf, 1, 2), ((0, 0), (0, FOX_HEADS_PAD - heads), (0, 0)))
    k_cache = cache_fox_k.reshape(dec_batch, past, fox_width)
    v_cache = cache_fox_v.reshape(dec_batch, past, fox_width)

    def fox_s(z, kvf, extras):
        if extras is None:
            k, v, zf, k16, v16 = kvf
            c, lf = logf_cumsum(_forget_inputs(zf, dec_batch, dec_seq, t_pad), b_col, dec_seq, prev_t)
            extras = (k, v, lf, c[:, :heads].reshape(dec_batch, heads, 1, past + t_pad), k16, v16)
        k, v, lf, c, k16, v16 = extras
        return fox_sample(z, k_cache, v_cache, k16, v16, c, batch=dec_batch, t_new=dec_seq, heads=heads), extras

    mem_k_s = [cache_mem_k[l].reshape(dec_batch, n_mem, MEM_WIDTH).astype(BF16) for l in range(depth)]
    mem_v_s = [cache_mem_v[l].reshape(dec_batch, n_mem, MEM_WIDTH).astype(BF16) for l in range(depth)]
    y_s, gmlp_v_s, ex_s = run_group(x_sample, mem_k_s, mem_v_s, min(dec_seq, GMLP_CHUNK), fox_s)
    fox_k_s = ex_s[0].reshape(dec_batch, dec_seq, heads, HEAD_DIM)
    fox_v_s = ex_s[1].reshape(dec_batch, dec_seq, heads, HEAD_DIM)
    fox_logf_s = jnp.swapaxes(ex_s[2][:, :heads, :dec_seq], 1, 2)

    return (y_p, y_s, new_mem_k, new_mem_v, fox_k_p, fox_v_p, fox_logf_p,
            fox_k_s, fox_v_s, fox_logf_s, jnp.stack(gmlp_v_s))
```
